```python
import jax, jax.numpy as jnp
from jax import lax
import numpy as np

D_MODEL = 2048
BATCH = 4
SEQ = 2048
DEPTH = 4
DEC_BATCH = 8
DEC_SEQ = 1
PAST_LEN = 16384
PAGE_SIZE = 128

D_MIX = D_MODEL
D_ATTN = D_MIX // 2
D_SGU = D_MIX - D_ATTN
HEAD_DIM = 128
N_ATTN_HEADS = D_ATTN // HEAD_DIM
SGU_GROUP_DIM = 128
N_SGU_GROUPS = D_SGU // SGU_GROUP_DIM
CHUNK = 128
Q_BLOCK = 128
D_FF = 256 * ((8 * D_MODEL // 3 + 255) // 256)
N_EXPERTS = 8
TOP_K = 2
D_FF_EXPERT = 7 * D_MODEL // 2
D_PLE = 256
EPS = 1e-6
ATTN_SCALE = HEAD_DIM ** -0.5
FORGET_BIAS_INIT = 4.0
D_IN = 3 * D_ATTN + N_ATTN_HEADS + 2 * D_SGU
SPLITS = (D_ATTN, 2 * D_ATTN, 3 * D_ATTN, 3 * D_ATTN + N_ATTN_HEADS, 3 * D_ATTN + N_ATTN_HEADS + D_SGU)

kernel_name = 'hybrid_fox_sgu_decoder_step'


def rms_norm(x, g):
    xf = x.astype(jnp.float32)
    y = xf * lax.rsqrt(jnp.mean(xf * xf, axis=-1, keepdims=True) + EPS)
    return (y * g.astype(jnp.float32)).astype(x.dtype)


def swiglu(x, wg, wu, wd):
    return (jax.nn.silu(x @ wg) * (x @ wu)) @ wd


def moe_swiglu(x, w_r, b_r, wg, wu, wd):
    shp = x.shape
    xt = x.reshape(-1, shp[-1])
    logits = (xt @ w_r).astype(jnp.float32) + b_r.astype(jnp.float32)
    top_v, top_i = lax.top_k(logits, TOP_K)
    gates = jax.nn.softmax(top_v, axis=-1)
    dense_gate = jnp.einsum('nk,nke->ne', gates, jax.nn.one_hot(top_i, N_EXPERTS, dtype=jnp.float32))
    y = jnp.zeros(xt.shape, jnp.float32)
    for e in range(N_EXPERTS):
        y = y + dense_gate[:, e:e + 1] * swiglu(xt, wg[e], wu[e], wd[e]).astype(jnp.float32)
    return y.astype(x.dtype).reshape(shp)


def mixer_inputs(h, g_mix, w_in, b_f, g_q, g_k, g_sgu):
    B, S, _ = h.shape
    z = rms_norm(h, g_mix) @ w_in
    q, k, v, f, u, vs = jnp.split(z, SPLITS, axis=-1)
    q = rms_norm(q.reshape(B, S, N_ATTN_HEADS, HEAD_DIM), g_q)
    k = rms_norm(k.reshape(B, S, N_ATTN_HEADS, HEAD_DIM), g_k)
    v = v.reshape(B, S, N_ATTN_HEADS, HEAD_DIM)
    logf = jax.nn.log_sigmoid(f.astype(jnp.float32) + b_f.astype(jnp.float32))
    u = jax.nn.gelu(u)
    vs = rms_norm(jax.nn.gelu(vs), g_sgu).reshape(B, S, N_SGU_GROUPS, SGU_GROUP_DIM)
    return q, k, v, logf, u, vs


def fox_prompt(q, k, v, logf):
    B, S, H, Dh = q.shape
    nb = S // Q_BLOCK
    ct = jnp.cumsum(logf, axis=1).transpose(0, 2, 1)
    qb = q.reshape(B, nb, Q_BLOCK, H, Dh).transpose(1, 0, 2, 3, 4)
    cb = ct.reshape(B, H, nb, Q_BLOCK).transpose(2, 0, 1, 3)
    kpos = jnp.arange(S)

    def block(args):
        qi, ci, j = args
        s = jnp.einsum('bqhd,bkhd->bhqk', qi, k).astype(jnp.float32) * ATTN_SCALE
        s = s + (ci[..., :, None] - ct[:, :, None, :])
        qpos = j * Q_BLOCK + jnp.arange(Q_BLOCK)
        s = jnp.where(kpos[None, :] <= qpos[:, None], s, -jnp.inf)
        p = jax.nn.softmax(s, axis=-1).astype(v.dtype)
        return jnp.einsum('bhqk,bkhd->bqhd', p, v)

    out = lax.map(block, (qb, cb, jnp.arange(nb)))
    return out.transpose(1, 0, 2, 3, 4).reshape(B, S, H * Dh)


def fox_sample(q, k, v, logf, k_past, v_past, logf_past):
    Bd, T, H, Dh = q.shape
    P = k_past.shape[1]
    c = jnp.cumsum(jnp.concatenate([logf_past.astype(jnp.float32), logf], axis=1), axis=1)
    c = c.transpose(0, 2, 1)
    s = jnp.concatenate([jnp.einsum('bqhd,bkhd->bhqk', q, k_past),
                         jnp.einsum('bqhd,bkhd->bhqk', q, k)], axis=-1).astype(jnp.float32) * ATTN_SCALE
    s = s + (c[:, :, P:, None] - c[:, :, None, :])
    mask = jnp.concatenate([jnp.ones((T, P), bool), jnp.tril(jnp.ones((T, T), bool))], axis=1)
    s = jnp.where(mask, s, -jnp.inf)
    p = jax.nn.softmax(s, axis=-1).astype(v.dtype)
    out = jnp.einsum('bhqk,bkhd->bqhd', p[..., :P], v_past) + jnp.einsum('bhqk,bkhd->bqhd', p[..., P:], v)
    return out.reshape(Bd, T, H * Dh)


def sgu_prompt(u, vs, w_s, b_s):
    B, S, G, C = vs.shape
    wm = (w_s * jnp.tril(jnp.ones((CHUNK, CHUNK), w_s.dtype))).astype(vs.dtype)
    vc = vs.reshape(B, S // CHUNK, CHUNK, G, C)
    z = jnp.einsum('gts,bnsgc->bntgc', wm, vc) + b_s.T[None, None, :, :, None]
    return u * z.reshape(B, S, G * C).astype(u.dtype)


def sgu_sample(u, vs, w_s, b_s):
    Bd, T, G, C = vs.shape
    wm = (w_s * jnp.tril(jnp.ones((CHUNK, CHUNK), w_s.dtype)))[:, :T, :T].astype(vs.dtype)
    z = jnp.einsum('gts,bsgc->btgc', wm, vs) + b_s[:, :T].T[None, :, :, None]
    return u * z.reshape(Bd, T, G * C).astype(u.dtype)


def mixer_output(h, att, sg, g_out_a, g_out_b, w_out):
    o = jnp.concatenate([rms_norm(att, g_out_a), rms_norm(sg, g_out_b)], axis=-1) @ w_out
    return h + o


def channel_and_ple(h, p_i, i, g_ffn, w_dense_gate, w_dense_up, w_dense_down, w_router, b_router,
                    w_moe_gate, w_moe_up, w_moe_down, g_pe, w_pe, w_pg):
    c = rms_norm(h, g_ffn[i])
    j = i // 2
    if i % 2 == 0:
        ff = swiglu(c, w_dense_gate[j], w_dense_up[j], w_dense_down[j])
    else:
        ff = moe_swiglu(c, w_router[j], b_router[j], w_moe_gate[j], w_moe_up[j], w_moe_down[j])
    h = h + ff
    gate = jax.nn.sigmoid((rms_norm(h, g_pe[i]) @ w_pg[i]).astype(jnp.float32))
    emb = (p_i.astype(h.dtype) @ w_pe[i]).astype(jnp.float32)
    return h + (emb * gate).astype(h.dtype)


def setup_inputs(seed: int = 0) -> dict:
    key = jax.random.key(seed)
    ks = jax.random.split(key, 32)
    f32 = jnp.float32

    def nrm(k, shape, scale=1.0):
        return scale * jax.random.normal(k, shape, f32)

    def gain(k, shape):
        return 1.0 + 0.05 * jax.random.normal(k, shape, f32)

    n_pages = PAST_LEN // PAGE_SIZE
    n_used = DEC_BATCH * n_pages
    n_pool = n_used + max(1, n_used // 4)
    n_dense = (DEPTH + 1) // 2
    n_moe = DEPTH // 2
    page_table = jax.random.permutation(ks[0], n_pool)[:n_used].reshape(DEC_BATCH, n_pages).astype(jnp.int32)
    return {
        'x_prompt': nrm(ks[1], (BATCH, SEQ, D_MODEL)),
        'x_sample': nrm(ks[2], (DEC_BATCH, DEC_SEQ, D_MODEL)),
        'cache_k': nrm(ks[3], (DEPTH, n_pool, PAGE_SIZE, N_ATTN_HEADS, HEAD_DIM)),
        'cache_v': nrm(ks[4], (DEPTH, n_pool, PAGE_SIZE, N_ATTN_HEADS, HEAD_DIM)),
        'cache_logf': jax.nn.log_sigmoid(FORGET_BIAS_INIT + nrm(ks[5], (DEPTH, n_pool, PAGE_SIZE, N_ATTN_HEADS))),
        'page_table': page_table,
        'p_prompt': nrm(ks[6], (DEPTH, BATCH, SEQ, D_PLE)),
        'p_sample': nrm(ks[7], (DEPTH, DEC_BATCH, DEC_SEQ, D_PLE)),
        'g_mix': gain(ks[8], (DEPTH, D_MODEL)),
        'w_in': nrm(ks[9], (DEPTH, D_MODEL, D_IN), D_MODEL ** -0.5),
        'b_f': FORGET_BIAS_INIT + nrm(ks[10], (DEPTH, N_ATTN_HEADS), 0.5),
        'g_q': gain(ks[11], (DEPTH, HEAD_DIM)),
        'g_k': gain(ks[12], (DEPTH, HEAD_DIM)),
        'w_s': nrm(ks[13], (DEPTH, N_SGU_GROUPS, CHUNK, CHUNK), CHUNK ** -0.5),
        'b_s': 1.0 + nrm(ks[14], (DEPTH, N_SGU_GROUPS, CHUNK), 0.02),
        'g_sgu': gain(ks[15], (DEPTH, D_SGU)),
        'g_out_a': gain(ks[16], (DEPTH, D_ATTN)),
        'g_out_b': gain(ks[17], (DEPTH, D_SGU)),
        'w_out': nrm(ks[18], (DEPTH, D_MIX, D_MODEL), D_MIX ** -0.5),
        'g_ffn': gain(ks[19], (DEPTH, D_MODEL)),
        'w_dense_gate': nrm(ks[20], (n_dense, D_MODEL, D_FF), D_MODEL ** -0.5),
        'w_dense_up': nrm(ks[21], (n_dense, D_MODEL, D_FF), D_MODEL ** -0.5),
        'w_dense_down': nrm(ks[22], (n_dense, D_FF, D_MODEL), D_FF ** -0.5),
        'w_router': nrm(ks[23], (n_moe, D_MODEL, N_EXPERTS), D_MODEL ** -0.5),
        'b_router': nrm(ks[24], (n_moe, N_EXPERTS), 0.01),
        'w_moe_gate': nrm(ks[25], (n_moe, N_EXPERTS, D_MODEL, D_FF_EXPERT), D_MODEL ** -0.5),
        'w_moe_up': nrm(ks[26], (n_moe, N_EXPERTS, D_MODEL, D_FF_EXPERT), D_MODEL ** -0.5),
        'w_moe_down': nrm(ks[27], (n_moe, N_EXPERTS, D_FF_EXPERT, D_MODEL), D_FF_EXPERT ** -0.5),
        'g_pe': gain(ks[28], (DEPTH, D_MODEL)),
        'w_pe': nrm(ks[29], (DEPTH, D_PLE, D_MODEL), D_PLE ** -0.5),
        'w_pg': nrm(ks[30], (DEPTH, D_MODEL, D_MODEL), D_MODEL ** -0.5),
    }


def reference(x_prompt, x_sample, cache_k, cache_v, cache_logf, page_table, p_prompt, p_sample,
              g_mix, w_in, b_f, g_q, g_k, w_s, b_s, g_sgu, g_out_a, g_out_b, w_out, g_ffn,
              w_dense_gate, w_dense_up, w_dense_down, w_router, b_router, w_moe_gate, w_moe_up,
              w_moe_down, g_pe, w_pe, w_pg):
    n_dec, n_pages = page_table.shape
    past_len = n_pages * cache_k.shape[2]
    hp, hs = x_prompt, x_sample
    kp_l, vp_l, lp_l, ks_l, vs_l, ls_l, us_l = [], [], [], [], [], [], []
    for i in range(DEPTH):
        q, k, v, lf, u, vsg = mixer_inputs(hp, g_mix[i], w_in[i], b_f[i], g_q[i], g_k[i], g_sgu[i])
        att = fox_prompt(q, k, v, lf)
        sg = sgu_prompt(u, vsg, w_s[i], b_s[i])
        hp = mixer_output(hp, att, sg, g_out_a[i], g_out_b[i], w_out[i])
        hp = channel_and_ple(hp, p_prompt[i], i, g_ffn, w_dense_gate, w_dense_up, w_dense_down, w_router,
                             b_router, w_moe_gate, w_moe_up, w_moe_down, g_pe, w_pe, w_pg)
        kp_l.append(k)
        vp_l.append(v)
        lp_l.append(lf.astype(cache_logf.dtype))
        q, k, v, lf, u, vsg = mixer_inputs(hs, g_mix[i], w_in[i], b_f[i], g_q[i], g_k[i], g_sgu[i])
        k_past = cache_k[i, page_table].reshape(n_dec, past_len, N_ATTN_HEADS, HEAD_DIM)
        v_past = cache_v[i, page_table].reshape(n_dec, past_len, N_ATTN_HEADS, HEAD_DIM)
        lf_past = cache_logf[i, page_table].reshape(n_dec, past_len, N_ATTN_HEADS)
        att = fox_sample(q, k, v, lf, k_past, v_past, lf_past)
        sg = sgu_sample(u, vsg, w_s[i], b_s[i])
        hs = mixer_output(hs, att, sg, g_out_a[i], g_out_b[i], w_out[i])
        hs = channel_and_ple(hs, p_sample[i], i, g_ffn, w_dense_gate, w_dense_up, w_dense_down, w_router,
                             b_router, w_moe_gate, w_moe_up, w_moe_down, g_pe, w_pe, w_pg)
        ks_l.append(k)
        vs_l.append(v)
        ls_l.append(lf.astype(cache_logf.dtype))
        us_l.append(vsg.reshape(vsg.shape[0], vsg.shape[1], D_SGU))
    k_prompt = jnp.stack(kp_l)
    v_prompt = jnp.stack(vp_l)
    logf_prompt = jnp.stack(lp_l)
    k_sample = jnp.stack(ks_l)
    v_sample = jnp.stack(vs_l)
    logf_sample = jnp.stack(ls_l)
    sgu_v_sample = jnp.stack(us_l)
    return (hp, hs, k_prompt, v_prompt, logf_prompt, k_sample, v_sample, logf_sample, sgu_v_sample)
```

```python
import functools

import jax
import jax.numpy as jnp
from jax import lax
from jax.experimental import pallas as pl
from jax.experimental.pallas import tpu as pltpu

F32 = jnp.float32
BF16 = jnp.bfloat16
I32 = jnp.int32

V7X_LANES = 128
V7X_SUBLANES = 8
V7X_VMEM_BYTES = 64 * 1024 * 1024
VMEM_LIMIT_BYTES = V7X_VMEM_BYTES - 8 * 1024 * 1024

HEAD_DIM = 128
GROUP_DIM = 128
CHUNK = 128
N_EXPERTS = 8
EPS = 1e-6
ATTN_SCALE = HEAD_DIM ** -0.5

ROW_TILE = 256
EXPERT_TILE = 256
GATHER_TILE = 256
PAGES_PER_STEP = 8

_NT = (((1,), (1,)), ((), ()))


def _params(*semantics):
    return pltpu.CompilerParams(dimension_semantics=semantics, vmem_limit_bytes=VMEM_LIMIT_BYTES)


def _col_tile(n, target):
    t = min(target, n)
    t -= t % V7X_LANES
    while n % t:
        t -= V7X_LANES
    return t


def _row_tile(n, target):
    t = min(target, n)
    t -= t % V7X_SUBLANES
    while n % t:
        t -= V7X_SUBLANES
    return t


def _rms(x, g):
    return x * lax.rsqrt(jnp.mean(x * x, axis=-1, keepdims=True) + EPS) * g


def _gelu(x):
    return 0.5 * x * (1.0 + jnp.tanh(0.7978845608028654 * (x + 0.044715 * (x * x * x))))


def _sigmoid(x):
    return 1.0 / (1.0 + jnp.exp(-x))


def _log_sigmoid(x):
    return jnp.minimum(x, 0.0) - jnp.log1p(jnp.exp(-jnp.abs(x)))


def _split3(x):
    x1 = x.astype(BF16)
    r1 = x - x1.astype(F32)
    x2 = r1.astype(BF16)
    x3 = (r1 - x2.astype(F32)).astype(BF16)
    return x1, x2, x3


def _bf16_round(x):
    return x.astype(BF16).astype(F32)


def _rms_kernel(x_ref, g_ref, o_ref):
    o_ref[...] = _rms(x_ref[...], g_ref[...]).astype(o_ref.dtype)


def _rms_cast(x, g):
    rows, d = x.shape
    return pl.pallas_call(
        _rms_kernel,
        grid=(pl.cdiv(rows, ROW_TILE),),
        in_specs=[pl.BlockSpec((ROW_TILE, d), lambda i: (i, 0)),
                  pl.BlockSpec((1, d), lambda i: (0, 0))],
        out_specs=pl.BlockSpec((ROW_TILE, d), lambda i: (i, 0)),
        out_shape=jax.ShapeDtypeStruct((rows, d), BF16),
        compiler_params=_params("arbitrary"),
        name="rms_cast",
    )(x, g)


def _rms_pair_kernel(a_ref, b_ref, ga_ref, gb_ref, o_ref):
    da = a_ref.shape[1]
    o_ref[:, :da] = _rms(a_ref[...], ga_ref[...]).astype(o_ref.dtype)
    o_ref[:, da:] = _rms(b_ref[...], gb_ref[...]).astype(o_ref.dtype)


def _rms_pair_cast(a, b, ga, gb):
    rows, da = a.shape
    db = b.shape[1]
    return pl.pallas_call(
        _rms_pair_kernel,
        grid=(pl.cdiv(rows, ROW_TILE),),
        in_specs=[pl.BlockSpec((ROW_TILE, da), lambda i: (i, 0)),
                  pl.BlockSpec((ROW_TILE, db), lambda i: (i, 0)),
                  pl.BlockSpec((1, da), lambda i: (0, 0)),
                  pl.BlockSpec((1, db), lambda i: (0, 0))],
        out_specs=pl.BlockSpec((ROW_TILE, da + db), lambda i: (i, 0)),
        out_shape=jax.ShapeDtypeStruct((rows, da + db), BF16),
        compiler_params=_params("arbitrary"),
        name="rms_pair_cast",
    )(a, b, ga, gb)


def _cast_weight_once(w_ref, wb_ref):
    @pl.when(pl.program_id(1) == 0)
    def _():
        wb_ref[...] = w_ref[...].astype(BF16)


def _qkv_kernel(a_ref, w_ref, g_ref, of_ref, ob_ref, wb_ref):
    _cast_weight_once(w_ref, wb_ref)
    z = jnp.dot(a_ref[...], wb_ref[...], preferred_element_type=F32)
    j = pl.program_id(0)

    @pl.when(j < 2)
    def _():
        g = g_ref[...]
        for h in range(z.shape[1] // HEAD_DIM):
            sl = slice(h * HEAD_DIM, (h + 1) * HEAD_DIM)
            y = _rms(z[:, sl], g)
            of_ref[:, sl] = y
            ob_ref[:, sl] = y.astype(BF16)

    @pl.when(j == 2)
    def _():
        of_ref[...] = z
        ob_ref[...] = z.astype(BF16)


def _qkv_proj(a, w_in, layer, gains, d_attn):
    rows, d = a.shape
    out =jax.ShapeDtypeStruct((3, rows, d_attn), F32), jax.ShapeDtypeStruct((3, rows, d_attn), BF16)
    return pl.pallas_call(
        _qkv_kernel,
        grid=(3, pl.cdiv(rows, ROW_TILE)),
        in_specs=[pl.BlockSpec((ROW_TILE, d), lambda j, i: (i, 0)),
                  pl.BlockSpec((None, d, d_attn), lambda j, i: (layer, 0, j)),
                  pl.BlockSpec((None, 1, HEAD_DIM), lambda j, i: (j, 0, 0))],
        out_specs=[pl.BlockSpec((None, ROW_TILE, d_attn), lambda j, i: (j, i, 0)),
                   pl.BlockSpec((None, ROW_TILE, d_attn), lambda j, i: (j, i, 0))],
        out_shape=out,
        scratch_shapes=[pltpu.VMEM((d, d_attn), BF16)],
        compiler_params=_params("arbitrary", "arbitrary"),
        name="qkv_proj",
    )(a, w_in, gains)


def _uvs_kernel(a_ref, w_ref, g_ref, o_ref, wb_ref):
    _cast_weight_once(w_ref, wb_ref)
    x = _gelu(jnp.dot(a_ref[...], wb_ref[...], preferred_element_type=F32))
    j = pl.program_id(0)

    @pl.when(j == 0)
    def _():
        o_ref[...] = x

    @pl.when(j == 1)
    def _():
        o_ref[...] = _rms(x, g_ref[...])


def _uvs_proj(a, w_uvs, layer, g_sgu):
    rows, d = a.shape
    d_sgu = w_uvs.shape[2] // 2
    return pl.pallas_call(
        _uvs_kernel,
        grid=(2, pl.cdiv(rows, ROW_TILE)),
        in_specs=[pl.BlockSpec((ROW_TILE, d), lambda j, i: (i, 0)),
                  pl.BlockSpec((None, d, d_sgu), lambda j, i: (layer, 0, j)),
                  pl.BlockSpec((1, d_sgu), lambda j, i: (0, 0))],
        out_specs=pl.BlockSpec((None, ROW_TILE, d_sgu), lambda j, i: (j, i, 0)),
        out_shape=jax.ShapeDtypeStruct((2, rows, d_sgu), F32),
        scratch_shapes=[pltpu.VMEM((d, d_sgu), BF16)],
        compiler_params=_params("arbitrary", "arbitrary"),
        name="uvs_proj",
    )(a, w_uvs, g_sgu)


def _mm_res_kernel(a_ref, w_ref, r_ref, o_ref, wb_ref):
    _cast_weight_once(w_ref, wb_ref)
    o_ref[...] = r_ref[...] + jnp.dot(a_ref[...], wb_ref[...], preferred_element_type=F32)


def _mm_residual(a, w, layer, res, name, col_target=512):
    rows, k = a.shape
    n = w.shape[2]
    tn = _col_tile(n, col_target)
    return pl.pallas_call(
        _mm_res_kernel,
        grid=(n // tn, pl.cdiv(rows, ROW_TILE)),
        in_specs=[pl.BlockSpec((ROW_TILE, k), lambda j, i: (i, 0)),
                  pl.BlockSpec((None, k, tn), lambda j, i: (layer, 0, j)),
                  pl.BlockSpec((ROW_TILE, tn), lambda j, i: (i, j))],
        out_specs=pl.BlockSpec((ROW_TILE, tn), lambda j, i: (i, j)),
        out_shape=jax.ShapeDtypeStruct((rows, n), F32),
        scratch_shapes=[pltpu.VMEM((k, tn), BF16)],
        compiler_params=_params("arbitrary", "arbitrary"),
        name=name,
    )(a, w, res)


def _ffn_up_kernel(a_ref, wg_ref, wu_ref, o_ref, wgb_ref, wub_ref):
    _cast_weight_once(wg_ref, wgb_ref)
    _cast_weight_once(wu_ref, wub_ref)
    a = a_ref[...]
    g = jnp.dot(a, wgb_ref[...], preferred_element_type=F32)
    u = jnp.dot(a, wub_ref[...], preferred_element_type=F32)
    o_ref[...] = (g * _sigmoid(g) * u).astype(o_ref.dtype)


def _ffn_up(a, wg, wu, layer):
    rows, k = a.shape
    n = wg.shape[2]
    tn = _col_tile(n, 512)
    wspec = pl.BlockSpec((None, k, tn), lambda j, i: (layer, 0, j))
    return pl.pallas_call(
        _ffn_up_kernel,
        grid=(n // tn, pl.cdiv(rows, ROW_TILE)),
        in_specs=[pl.BlockSpec((ROW_TILE, k), lambda j, i: (i, 0)), wspec, wspec],
        out_specs=pl.BlockSpec((ROW_TILE, tn), lambda j, i: (i, j)),
        out_shape=jax.ShapeDtypeStruct((rows, n), BF16),
        scratch_shapes=[pltpu.VMEM((k, tn), BF16), pltpu.VMEM((k, tn), BF16)],
        compiler_params=_params("arbitrary", "arbitrary"),
        name="ffn_up",
    )(a, wg, wu)


def _ple_kernel(a_ref, w_ref, p_ref, wpe_ref, r_ref, o_ref, wb_ref):
    _cast_weight_once(w_ref, wb_ref)
    gate = _sigmoid(jnp.dot(a_ref[...], wb_ref[...], preferred_element_type=F32))
    emb = jnp.dot(p_ref[...].astype(BF16), wpe_ref[...].astype(BF16), preferred_element_type=F32)
    o_ref[...] = r_ref[...] + emb * gate


def _ple(a, w_pg, p, w_pe, layer, res):
    rows, k = a.shape
    n = w_pg.shape[2]
    d_ple = p.shape[1]
    tn = _col_tile(n, 512)
    return pl.pallas_call(
        _ple_kernel,
        grid=(n // tn, pl.cdiv(rows, ROW_TILE)),
        in_specs=[pl.BlockSpec((ROW_TILE, k), lambda j, i: (i, 0)),
                  pl.BlockSpec((None, k, tn), lambda j, i: (layer, 0, j)),
                  pl.BlockSpec((ROW_TILE, d_ple), lambda j, i: (i, 0)),
                  pl.BlockSpec((None, d_ple, tn), lambda j, i: (layer, 0, j)),
                  pl.BlockSpec((ROW_TILE, tn), lambda j, i: (i, j))],
        out_specs=pl.BlockSpec((ROW_TILE, tn), lambda j, i: (i, j)),
        out_shape=jax.ShapeDtypeStruct((rows, n), F32),
        scratch_shapes=[pltpu.VMEM((k, tn), BF16)],
        compiler_params=_params("arbitrary", "arbitrary"),
        name="ple",
    )(a, w_pg, p, w_pe, res)


def _forget_kernel(a_ref, wf_ref, wft_ref, bf_ref, bft_ref, lf_ref, cc_ref, cr_ref, carc_ref, carr_ref, *, seq):
    i = pl.program_id(0)
    tm = a_ref.shape[0]
    n_h = lf_ref.shape[1]
    a = a_ref[...]
    lf = _log_sigmoid(jnp.dot(a, wf_ref[...].astype(BF16), preferred_element_type=F32) + bf_ref[...])
    lft = _log_sigmoid(lax.dot_general(wft_ref[...].astype(BF16), a, _NT, preferred_element_type=F32)
                       + bft_ref[...])

    @pl.when((i * tm) % seq == 0)
    def _():
        carc_ref[...] = jnp.zeros_like(carc_ref)
        carr_ref[...] = jnp.zeros_like(carr_ref)

    row = lax.broadcasted_iota(I32, (tm, tm), 0)
    col = lax.broadcasted_iota(I32, (tm, tm), 1)
    lower = (col <= row).astype(BF16)
    upper = (row <= col).astype(BF16)
    c = carc_ref[...]
    for piece in _split3(lf):
        c = c + jnp.dot(lower, piece, preferred_element_type=F32)
    ct = carr_ref[...]
    for piece in _split3(lft):
        ct = ct + jnp.dot(piece, upper, preferred_element_type=F32)
    carc_ref[...] = c[tm - 1:tm, :]
    carr_ref[...] = ct[:, tm - 1:tm]
    lf_ref[...] = lf[:, :n_h]
    cc_ref[...] = c[:, :n_h]
    cr_ref[...] = ct


def _forget_gates(a, wf_pad, wf_t, bf_pad, bf_t, seq):
    rows, d = a.shape
    n_h = wf_t.shape[0]
    n_tiles = pl.cdiv(rows, ROW_TILE)
    return pl.pallas_call(
        functools.partial(_forget_kernel, seq=seq),
        grid=(n_tiles,),
        in_specs=[pl.BlockSpec((ROW_TILE, d), lambda i: (i, 0)),
                  pl.BlockSpec((d, V7X_LANES), lambda i: (0, 0)),
                  pl.BlockSpec((n_h, d), lambda i: (0, 0)),
                  pl.BlockSpec((1, V7X_LANES), lambda i: (0, 0)),
                  pl.BlockSpec((n_h, 1), lambda i: (0, 0))],
        out_specs=[pl.BlockSpec((ROW_TILE, n_h), lambda i: (i, 0)),
                   pl.BlockSpec((ROW_TILE, n_h), lambda i: (i, 0)),
                   pl.BlockSpec((None, n_h, ROW_TILE), lambda i: (i, 0, 0))],
        out_shape=(jax.ShapeDtypeStruct((rows, n_h), F32),
                   jax.ShapeDtypeStruct((rows, n_h), F32),
                   jax.ShapeDtypeStruct((n_tiles, n_h, ROW_TILE), F32)),
        scratch_shapes=[pltpu.VMEM((1, V7X_LANES), F32), pltpu.VMEM((n_h, 1), F32)],
        compiler_params=_params("arbitrary"),
        name="forget_gates",
    )(a, wf_pad, wf_t, bf_pad, bf_t)


def _sgu_kernel(u_ref, v_ref, w_ref, b_ref, o_ref, *, n_prompt_chunks):
    c = pl.program_id(0)
    n_groups = w_ref.shape[0]
    row = lax.broadcasted_iota(I32, (CHUNK, CHUNK), 0)
    col = lax.broadcasted_iota(I32, (CHUNK, CHUNK), 1)

    @pl.when(c < n_prompt_chunks)
    def _():
        for g in range(n_groups):
            sl = slice(g * GROUP_DIM, (g + 1) * GROUP_DIM)
            wm = jnp.where(col <= row, w_ref[g], 0.0).astype(BF16)
            z = jnp.dot(wm, v_ref[:, sl].astype(BF16), preferred_element_type=F32) + b_ref[:, g:g + 1]
            o_ref[:, sl] = u_ref[:, sl] * z

    @pl.when(c >= n_prompt_chunks)
    def _():
        for g in range(n_groups):
            sl = slice(g * GROUP_DIM, (g + 1) * GROUP_DIM)
            w00 = _bf16_round(w_ref[g][0:1, 0:1])
            z = w00 * _bf16_round(v_ref[:, sl]) + b_ref[0:1, g:g + 1]
            o_ref[:, sl] = u_ref[:, sl] * z


def _sgu(uvs, w_s, b_s_t, layer, n_prompt_rows):
    _, rows, d_sgu = uvs.shape
    n_groups = w_s.shape[1]
    return pl.pallas_call(
        functools.partial(_sgu_kernel, n_prompt_chunks=n_prompt_rows // CHUNK),
        grid=(rows // CHUNK,),
        in_specs=[pl.BlockSpec((None, CHUNK, d_sgu), lambda c: (0, c, 0)),
                  pl.BlockSpec((None, CHUNK, d_sgu), lambda c: (1, c, 0)),
                  pl.BlockSpec((None, n_groups, CHUNK, CHUNK), lambda c: (layer, 0, 0, 0)),
                  pl.BlockSpec((None, CHUNK, n_groups), lambda c: (layer, 0, 0))],
        out_specs=pl.BlockSpec((CHUNK, d_sgu), lambda c: (c, 0)),
        out_shape=jax.ShapeDtypeStruct((rows, d_sgu), F32),
        compiler_params=_params("arbitrary"),
        name="sgu",
    )(uvs, uvs, w_s, b_s_t)


def _attn_kernel(q_ref, k_ref, v_ref, cc_ref, cr_ref, o_ref):
    qi = pl.program_id(1)
    ts = q_ref.shape[0]
    n_heads = q_ref.shape[1] // HEAD_DIM
    row = lax.broadcasted_iota(I32, (ts, ts), 0)
    col = lax.broadcasted_iota(I32, (ts, ts), 1)
    for h in range(n_heads):
        sl = slice(h * HEAD_DIM, (h + 1) * HEAD_DIM)
        q = q_ref[:, sl]
        cq = cc_ref[:, h:h + 1]

        def tile(kt, carry, masked):
            m, l, acc = carry
            off = pl.multiple_of(kt * ts, ts)
            s = lax.dot_general(q, k_ref[pl.ds(off, ts), sl], _NT, preferred_element_type=F32)
            s = s + (cq - cr_ref[kt, h:h + 1, :])
            if masked:
                s = jnp.where(col <= row, s, -jnp.inf)
            m_new = jnp.maximum(m, jnp.max(s, axis=-1, keepdims=True))
            alpha = jnp.exp(m - m_new)
            p = jnp.exp(s - m_new)
            l = alpha * l + jnp.sum(p, axis=-1, keepdims=True)
            acc = alpha * acc + jnp.dot(p.astype(BF16), v_ref[pl.ds(off, ts), sl], preferred_element_type=F32)
            return m_new, l, acc

        init = (jnp.full((ts, 1), -jnp.inf, F32), jnp.zeros((ts, 1), F32), jnp.zeros((ts, HEAD_DIM), F32))
        carry = lax.fori_loop(0, qi, lambda kt, c: tile(kt, c, False), init)
        _, l, acc = tile(qi, carry, True)
        o_ref[:, sl] = acc / l


def _prompt_attention(qkv_b, c_col, c_row, batch, seq):
    d_attn = qkv_b.shape[2]
    n_heads = d_attn // HEAD_DIM
    ts = ROW_TILE
    nq = seq // ts
    return pl.pallas_call(
        _attn_kernel,
        grid=(batch, nq),
        in_specs=[pl.BlockSpec((None, ts, d_attn), lambda b, qi: (0, b * nq + qi, 0)),
                  pl.BlockSpec((None, seq, d_attn), lambda b, qi: (1, b, 0)),
                  pl.BlockSpec((None, seq, d_attn), lambda b, qi: (2, b, 0)),
                  pl.BlockSpec((ts, n_heads), lambda b, qi: (b * nq + qi, 0)),
                  pl.BlockSpec((nq, n_heads, ts), lambda b, qi: (b, 0, 0))],
        out_specs=pl.BlockSpec((ts, d_attn), lambda b, qi: (b * nq + qi, 0)),
        out_shape=jax.ShapeDtypeStruct((batch * seq, d_attn), F32),
        compiler_params=_params("arbitrary", "arbitrary"),
        name="prompt_attention",
    )(qkv_b, qkv_b, qkv_b, c_col, c_row)


def _page_suffix_kernel(lf_ref, o_ref):
    page = lf_ref.shape[1]
    later = (lax.broadcasted_iota(I32, (page, page), 0) > lax.broadcasted_iota(I32, (page, page), 1)).astype(BF16)
    acc = jnp.zeros(o_ref.shape, F32)
    for piece in _split3(lf_ref[...]):
        acc = acc + jnp.dot(piece, later, preferred_element_type=F32)
    o_ref[...] = acc


def _page_suffix(lf_t):
    depth, rows, page = lf_t.shape
    tile = _row_tile(rows, 1024)
    spec = pl.BlockSpec((None, tile, page), lambda d, i: (d, i, 0))
    return pl.pallas_call(
        _page_suffix_kernel,
        grid=(depth, rows // tile),
        in_specs=[spec],
        out_specs=spec,
        out_shape=jax.ShapeDtypeStruct(lf_t.shape, F32),
        compiler_params=_params("arbitrary", "arbitrary"),
        name="page_suffix",
    )(lf_t)


def _decode_kernel(pt_ref, q_ref, kn_ref, vn_ref, lfn_ref, *rest, n_pp):
    del pt_ref
    k_refs, v_refs = rest[:n_pp], rest[n_pp:2 * n_pp]
    lf_refs, suf_refs = rest[2 * n_pp:3 * n_pp], rest[3 * n_pp:4 * n_pp]
    o_ref, m_ref, l_ref, acc_ref, car_ref = rest[4 * n_pp:]
    g = pl.program_id(1)
    n_h = q_ref.shape[0]
    width = k_refs[0].shape[0]
    own_head = (lax.rem(lax.broadcasted_iota(I32, (n_h, width), 1), n_h)
                == lax.broadcasted_iota(I32, (n_h, width), 0))
    lane_is_head = (lax.broadcasted_iota(I32, (n_h, V7X_LANES), 1)
                    == lax.broadcasted_iota(I32, (n_h, V7X_LANES), 0))
    q = q_ref[...]
    qb = q.astype(BF16)

    @pl.when(g == 0)
    def _():
        m_ref[...] = jnp.full_like(m_ref, -jnp.inf)
        l_ref[...] = jnp.zeros_like(l_ref)
        acc_ref[...] = jnp.zeros_like(acc_ref)
        car_ref[...] = lfn_ref[...]

    for r in range(n_pp):
        suf = suf_refs[r][...]
        car = car_ref[...]
        s = lax.dot_general(qb, k_refs[r][...].astype(BF16), _NT, preferred_element_type=F32)
        s = jnp.where(own_head, s + (suf + car), -jnp.inf)
        m = m_ref[...]
        m_new = jnp.maximum(m, jnp.max(s, axis=-1, keepdims=True))
        alpha = jnp.exp(m - m_new)
        p = jnp.exp(s - m_new)
        l_ref[...] = alpha * l_ref[...] + jnp.sum(p, axis=-1, keepdims=True)
        acc_ref[...] = alpha * acc_ref[...] + jnp.dot(p.astype(BF16), v_refs[r][...].astype(BF16),
                                                      preferred_element_type=F32)
        m_ref[...] = m_new
        page_mass = suf[:, :V7X_LANES] + lf_refs[r][:, :V7X_LANES]
        car_ref[...] = car + jnp.sum(jnp.where(lane_is_head, page_mass, 0.0), axis=-1, keepdims=True)

    @pl.when(g == pl.num_programs(1) - 1)
    def _():
        s_self = jnp.sum(q * _bf16_round(kn_ref[...]), axis=-1, keepdims=True)
        m = m_ref[...]
        m_new = jnp.maximum(m, s_self)
        alpha = jnp.exp(m - m_new)
        p_self = jnp.exp(s_self - m_new)
        l = alpha * l_ref[...] + p_self
        acc = alpha * acc_ref[...] + _bf16_round(p_self) * _bf16_round(vn_ref[...])
        o_ref[...] = acc / l


def _decode_attention(q, k_new, v_new, lf_new, cache_k, cache_v, lf_rows, suf_rows, page_table, layer):
    n_dec, n_h, head_dim = q.shape
    n_pages = page_table.shape[1]
    width = cache_k.shape[2]
    n_pp = PAGES_PER_STEP
    while n_pages % n_pp:
        n_pp //= 2

    def page_map(r):
        return lambda b, g, pt: (layer, pt[b * n_pages + (n_pages - 1 - (g * n_pp + r))], 0, 0)

    vec = pl.BlockSpec((None, n_h, head_dim), lambda b, g, pt: (b, 0, 0))
    in_specs = [vec, vec, vec, pl.BlockSpec((None, n_h, 1), lambda b, g, pt: (b, 0, 0))]
    in_specs += [pl.BlockSpec((None, None, width, head_dim), page_map(r)) for r in range(n_pp)] * 2
    in_specs += [pl.BlockSpec((None, None, 1, width), page_map(r)) for r in range(n_pp)] * 2
    grid_spec = pltpu.PrefetchScalarGridSpec(
        num_scalar_prefetch=1,
        grid=(n_dec, n_pages // n_pp),
        in_specs=in_specs,
        out_specs=vec,
        scratch_shapes=[pltpu.VMEM((n_h, 1), F32), pltpu.VMEM((n_h, 1), F32),
                        pltpu.VMEM((n_h, head_dim), F32), pltpu.VMEM((n_h, 1), F32)],
    )
    return pl.pallas_call(
        functools.partial(_decode_kernel, n_pp=n_pp),
        grid_spec=grid_spec,
        out_shape=jax.ShapeDtypeStruct((n_dec, n_h, head_dim), F32),
        compiler_params=_params("arbitrary", "arbitrary"),
        name="decode_attention",
    )(page_table.reshape(-1), q, k_new, v_new, lf_new,
      *([cache_k] * n_pp), *([cache_v] * n_pp), *([lf_rows] * n_pp), *([suf_rows] * n_pp))


def _route_kernel(h_ref, g_ref, wr_ref, br_ref, c_ref, idx_ref, gate_ref, cnt_ref, *, n_valid_rows):
    i = pl.program_id(0)
    tm = h_ref.shape[0]
    lanes = wr_ref.shape[1]
    c = _rms(h_ref[...], g_ref[...])
    c_ref[...] = c
    logits = jnp.dot(c, wr_ref[...], preferred_element_type=F32, precision=lax.Precision.HIGHEST) + br_ref[...]
    lane = lax.broadcasted_iota(I32, (tm, lanes), 1)
    lg = jnp.where(lane < N_EXPERTS, logits, -jnp.inf)
    m1 = jnp.max(lg, axis=-1, keepdims=True)
    i1 = jnp.min(jnp.where(lg == m1, lane, lanes), axis=-1, keepdims=True)
    lg2 = jnp.where(lane == i1, -jnp.inf, lg)
    m2 = jnp.max(lg2, axis=-1, keepdims=True)
    i2 = jnp.min(jnp.where(lg2 == m2, lane, lanes), axis=-1, keepdims=True)
    e2 = jnp.exp(m2 - m1)
    valid = (i * tm + lax.broadcasted_iota(I32, (tm, 1), 0)) < n_valid_rows
    g1 = jnp.where(valid, 1.0 / (1.0 + e2), 0.0)
    g2 = jnp.where(valid, e2 / (1.0 + e2), 0.0)

    @pl.when(i == 0)
    def _():
        cnt_ref[...] = jnp.zeros_like(cnt_ref)

    sel = jnp.where(((lane == i1) | (lane == i2)) & valid, 1.0, 0.0)
    before = (lax.broadcasted_iota(I32, (tm, tm), 1) < lax.broadcasted_iota(I32, (tm, tm), 0)).astype(BF16)
    cnt = cnt_ref[...]
    rank = jnp.dot(before, sel.astype(BF16), preferred_element_type=F32) + cnt.astype(F32)
    cnt_ref[...] = cnt + jnp.sum(sel, axis=0, keepdims=True).astype(I32)
    r1 = jnp.sum(jnp.where(lane == i1, rank, 0.0), axis=-1, keepdims=True).astype(I32)
    r2 = jnp.sum(jnp.where(lane == i2, rank, 0.0), axis=-1, keepdims=True).astype(I32)
    out_lane = lax.broadcasted_iota(I32, idx_ref.shape, 1)
    idx_ref[...] = jnp.where(out_lane == 0, i1, jnp.where(out_lane == 1, i2, jnp.where(out_lane == 2, r1, r2)))
    gate_ref[...] = jnp.where(out_lane == 0, g1, jnp.where(out_lane == 1, g2, 0.0))


def _route(h, g, wr_pad, br_pad, n_valid_rows):
    rows, d = h.shape
    lanes = wr_pad.shape[1]
    return pl.pallas_call(
        functools.partial(_route_kernel, n_valid_rows=n_valid_rows),
        grid=(pl.cdiv(rows, ROW_TILE),),
        in_specs=[pl.BlockSpec((ROW_TILE, d), lambda i: (i, 0)),
                  pl.BlockSpec((1, d), lambda i: (0, 0)),
                  pl.BlockSpec((d, lanes), lambda i: (0, 0)),
                  pl.BlockSpec((1, lanes), lambda i: (0, 0))],
        out_specs=[pl.BlockSpec((ROW_TILE, d), lambda i: (i, 0)),
                   pl.BlockSpec((ROW_TILE, N_EXPERTS), lambda i: (i, 0)),
                   pl.BlockSpec((ROW_TILE, N_EXPERTS), lambda i: (i, 0)),
                   pl.BlockSpec((1, lanes), lambda i: (0, 0))],
        out_shape=(jax.ShapeDtypeStruct((rows, d), F32),
                   jax.ShapeDtypeStruct((rows, N_EXPERTS), I32),
                   jax.ShapeDtypeStruct((rows, N_EXPERTS), F32),
                   jax.ShapeDtypeStruct((1, lanes), I32)),
        compiler_params=_params("arbitrary"),
        name="route",
    )(h, g, wr_pad, br_pad)


def _row_copy(table_ref, dst_ref, row, r, sem):
    return pltpu.make_async_copy(table_ref.at[pl.ds(row, 1)], dst_ref.at[pl.ds(r, 1)], sem)


def _gather_kernel(idx_ref, table_ref, o_ref, buf_ref, sem):
    tg = o_ref.shape[0]
    base = pl.program_id(0) * tg

    def start(r, carry):
        _row_copy(table_ref, buf_ref, idx_ref[base + r], r, sem).start()
        return carry

    def wait(r, carry):
        _row_copy(table_ref, buf_ref, 0, r, sem).wait()
        return carry

    lax.fori_loop(0, tg, start, 0)
    lax.fori_loop(0, tg, wait, 0)
    o_ref[...] = buf_ref[...].astype(o_ref.dtype)


def _gather_rows(table, idx, out_dtype):
    n_out = idx.shape[0]
    d = table.shape[1]
    grid_spec = pltpu.PrefetchScalarGridSpec(
        num_scalar_prefetch=1,
        grid=(n_out // GATHER_TILE,),
        in_specs=[pl.BlockSpec(memory_space=pl.ANY)],
        out_specs=pl.BlockSpec((GATHER_TILE, d), lambda i, idx: (i, 0)),
        scratch_shapes=[pltpu.VMEM((GATHER_TILE, d), table.dtype), pltpu.SemaphoreType.DMA(())],
    )
    return pl.pallas_call(
        _gather_kernel,
        grid_spec=grid_spec,
        out_shape=jax.ShapeDtypeStruct((n_out, d), out_dtype),
        compiler_params=_params("arbitrary"),
        name="gather_rows",
    )(idx, table)


def _cast_expert_weight(te_ref, w_ref, wb_ref):
    i = pl.program_id(1)

    @pl.when((i == 0) | (te_ref[i] != te_ref[jnp.maximum(i - 1, 0)]))
    def _():
        wb_ref[...] = w_ref[...].astype(BF16)


def _moe_up_kernel(te_ref, nu_ref, x_ref, wg_ref, wu_ref, o_ref, wgb_ref, wub_ref):
    _cast_expert_weight(te_ref, wg_ref, wgb_ref)
    _cast_expert_weight(te_ref, wu_ref, wub_ref)
    used = pl.program_id(1) < nu_ref[0]

    @pl.when(used)
    def _():
        x = x_ref[...]
        g = jnp.dot(x, wgb_ref[...], preferred_element_type=F32)
        u = jnp.dot(x, wub_ref[...], preferred_element_type=F32)
        o_ref[...] = (g * _sigmoid(g) * u).astype(o_ref.dtype)

    @pl.when(jnp.logical_not(used))
    def _():
        o_ref[...] = jnp.zeros_like(o_ref)


def _moe_up(xs, wg, wu, layer, tile_expert, n_used):
    rows, k = xs.shape
    n = wg.shape[3]
    tn = _col_tile(n, 512)
    wspec = pl.BlockSpec((None, None, k, tn), lambda j, i, te, nu: (layer, te[i], 0, j))
    grid_spec = pltpu.PrefetchScalarGridSpec(
        num_scalar_prefetch=2,
        grid=(n // tn, rows // EXPERT_TILE),
        in_specs=[pl.BlockSpec((EXPERT_TILE, k), lambda j, i, te, nu: (jnp.minimum(i, nu[0] - 1), 0)),
                  wspec, wspec],
        out_specs=pl.BlockSpec((EXPERT_TILE, tn), lambda j, i, te, nu: (i, j)),
        scratch_shapes=[pltpu.VMEM((k, tn), BF16), pltpu.VMEM((k, tn), BF16)],
    )
    return pl.pallas_call(
        _moe_up_kernel,
        grid_spec=grid_spec,
        out_shape=jax.ShapeDtypeStruct((rows, n), BF16),
        compiler_params=_params("arbitrary", "arbitrary"),
        name="moe_up",
    )(tile_expert, n_used, xs, wg, wu)


def _moe_down_kernel(te_ref, nu_ref, x_ref, w_ref, o_ref, wb_ref):
    _cast_expert_weight(te_ref, w_ref, wb_ref)
    used = pl.program_id(1) < nu_ref[0]

    @pl.when(used)
    def _():
        o_ref[...] = jnp.dot(x_ref[...], wb_ref[...], preferred_element_type=F32)

    @pl.when(jnp.logical_not(used))
    def _():
        o_ref[...] = jnp.zeros_like(o_ref)


def _moe_down(hs, wd, layer, tile_expert, n_used):
    rows, k = hs.shape
    n = wd.shape[3]
    tn = _col_tile(n, 512)
    grid_spec = pltpu.PrefetchScalarGridSpec(
        num_scalar_prefetch=2,
        grid=(n // tn, rows // EXPERT_TILE),
        in_specs=[pl.BlockSpec((EXPERT_TILE, k), lambda j, i, te, nu: (jnp.minimum(i, nu[0] - 1), 0)),
                  pl.BlockSpec((None, None, k, tn), lambda j, i, te, nu: (layer, te[i], 0, j))],
        out_specs=pl.BlockSpec((EXPERT_TILE, tn), lambda j, i, te, nu: (i, j)),
        scratch_shapes=[pltpu.VMEM((k, tn), BF16)],
    )
    return pl.pallas_call(
        _moe_down_kernel,
        grid_spec=grid_spec,
        out_shape=jax.ShapeDtypeStruct((rows, n), F32),
        compiler_params=_params("arbitrary", "arbitrary"),
        name="moe_down",
    )(tile_expert, n_used, hs, wd)


def _combine_kernel(h_ref, y_ref, gate_ref, g_ref, oh_ref, on_ref):
    d = h_ref.shape[1]
    gate = gate_ref[...]
    h = h_ref[...] + gate[:, 0:1] * y_ref[:, :d] + gate[:, 1:2] * y_ref[:, d:]
    oh_ref[...] = h
    on_ref[...] = _rms(h, g_ref[...]).astype(on_ref.dtype)


def _combine(h, y_pairs, gates, g):
    rows, d = h.shape
    return pl.pallas_call(
        _combine_kernel,
        grid=(pl.cdiv(rows, ROW_TILE),),
        in_specs=[pl.BlockSpec((ROW_TILE, d), lambda i: (i, 0)),
                  pl.BlockSpec((ROW_TILE, 2 * d), lambda i: (i, 0)),
                  pl.BlockSpec((ROW_TILE, N_EXPERTS), lambda i: (i, 0)),
                  pl.BlockSpec((1, d), lambda i: (0, 0))],
        out_specs=[pl.BlockSpec((ROW_TILE, d), lambda i: (i, 0)),
                   pl.BlockSpec((ROW_TILE, d), lambda i: (i, 0))],
        out_shape=(jax.ShapeDtypeStruct((rows, d), F32), jax.ShapeDtypeStruct((rows, d), BF16)),
        compiler_params=_params("arbitrary"),
        name="moe_combine",
    )(h, y_pairs, gates, g)


def _moe_ffn(h, g_ffn, w_router, b_router, wg, wu, wd, layer, n_valid_rows, g_next):
    rows, d = h.shape
    wr_pad = jnp.pad(w_router[layer], ((0, 0), (0, V7X_LANES - N_EXPERTS)))
    br_pad = jnp.pad(b_router[layer], (0, V7X_LANES - N_EXPERTS))[None]
    c, idx, gates, counts = _route(h, g_ffn, wr_pad, br_pad, n_valid_rows)

    n_tiles_max = pl.cdiv(2 * n_valid_rows, EXPERT_TILE) + N_EXPERTS
    n_slots = n_tiles_max * EXPERT_TILE
    counts = counts[0, :N_EXPERTS]
    tiles = (counts + EXPERT_TILE - 1) // EXPERT_TILE
    tile_end = jnp.cumsum(tiles)
    tile_start = tile_end - tiles
    n_used = tile_end[-1:].astype(I32)
    tile_ids = jnp.minimum(jnp.arange(n_tiles_max, dtype=I32), n_used[0] - 1)
    tile_expert = jnp.minimum(jnp.searchsorted(tile_end, tile_ids, side="right"), N_EXPERTS - 1).astype(I32)
    valid = (jnp.arange(rows) < n_valid_rows)[:, None]
    slot = tile_start[idx[:, 0:2]] * EXPERT_TILE + idx[:, 2:4]
    token = jnp.broadcast_to(jnp.arange(rows, dtype=I32)[:, None], slot.shape)
    src = jnp.zeros((n_slots,), I32).at[jnp.where(valid, slot, n_slots).reshape(-1)].set(
        token.reshape(-1), mode="drop")

    xs = _gather_rows(c, src, BF16)
    hid = _moe_up(xs, wg, wu, layer, tile_expert, n_used)
    y = _moe_down(hid, wd, layer, tile_expert, n_used)
    y_pairs = _gather_rows(y, jnp.where(valid, slot, 0).reshape(-1).astype(I32), F32).reshape(rows, 2 * d)
    return _combine(h, y_pairs, gates, g_next)


def kernel(x_prompt, x_sample, cache_k, cache_v, cache_logf, page_table, p_prompt, p_sample,
           g_mix, w_in, b_f, g_q, g_k, w_s, b_s, g_sgu, g_out_a, g_out_b, w_out, g_ffn,
           w_dense_gate, w_dense_up, w_dense_down, w_router, b_router, w_moe_gate, w_moe_up,
           w_moe_down, g_pe, w_pe, w_pg):
    batch, seq, d_model = x_prompt.shape
    n_dec, dec_seq, _ = x_sample.shape
    assert dec_seq == 1, "the decode attention handles one new token per sequence"
    depth = w_in.shape[0]
    n_pool, page = cache_k.shape[1], cache_k.shape[2]
    n_heads = cache_k.shape[3]
    d_attn = n_heads * HEAD_DIM
    d_sgu = d_model - d_attn
    n_prompt = batch * seq
    n_rows = n_prompt + n_dec
    assert seq % ROW_TILE == 0 and n_prompt % CHUNK == 0
    rows = pl.cdiv(n_rows + CHUNK, ROW_TILE) * ROW_TILE

    h = jnp.concatenate([x_prompt.reshape(n_prompt, d_model), x_sample.reshape(n_dec, d_model),
                         jnp.zeros((rows - n_rows, d_model), F32)])
    p_all = jnp.concatenate([p_prompt.reshape(depth, n_prompt, -1), p_sample.reshape(depth, n_dec, -1),
                             jnp.zeros((depth, rows - n_rows, p_prompt.shape[-1]), F32)], axis=1)

    w_f =w_in[:, :, 3 * d_attn:3 * d_attn + n_heads]
    w_uvs = w_in[:, :, 3 * d_attn + n_heads:]
    wf_pad = jnp.pad(w_f, ((0, 0), (0, 0), (0, V7X_LANES - n_heads)))
    wf_t = jnp.swapaxes(w_f, 1, 2)
    bf_pad = jnp.pad(b_f, ((0, 0), (0, V7X_LANES - n_heads)))
    ones = jnp.ones((depth, HEAD_DIM), F32)
    qkv_gain = jnp.stack([g_q * ATTN_SCALE, g_k, ones], axis=1)[:, :, None, :]
    b_s_t = jnp.swapaxes(b_s, 1, 2)
    cache_k2 = cache_k.reshape(depth, n_pool, page * n_heads, HEAD_DIM)
    cache_v2 = cache_v.reshape(depth, n_pool, page * n_heads, HEAD_DIM)
    lf_rows = cache_logf.reshape(depth, n_pool, 1, page * n_heads)
    suf_t = _page_suffix(jnp.swapaxes(cache_logf, 2, 3).reshape(depth, n_pool * n_heads, page))
    suf_rows = jnp.swapaxes(suf_t.reshape(depth, n_pool, n_heads, page), 2, 3).reshape(depth, n_pool, 1, page * n_heads)

    outs = {name: [] for name in ("kp", "vp", "lp", "ks", "vs", "ls", "us")}
    a = _rms_cast(h, g_mix[0][None])
    for layer in range(depth):
        qkv_f, qkv_b = _qkv_proj(a, w_in, layer, qkv_gain[layer], d_attn)
        logf, c_col, c_row = _forget_gates(a, wf_pad[layer], wf_t[layer], bf_pad[layer][None],
                                           b_f[layer][:, None], seq)
        uvs = _uvs_proj(a, w_uvs, layer, g_sgu[layer][None])
        sg = _sgu(uvs, w_s, b_s_t, layer, n_prompt)
        att_p = _prompt_attention(qkv_b, c_col, c_row, batch, seq)
        dec = slice(n_prompt, n_rows)
        per_head = (n_dec, n_heads, HEAD_DIM)
        att_s = _decode_attention(qkv_b[0, dec].astype(F32).reshape(per_head), qkv_f[1, dec].reshape(per_head),
                                  qkv_f[2, dec].reshape(per_head), logf[dec][:, :, None],
                                  cache_k2, cache_v2, lf_rows, suf_rows, page_table, layer)
        att = jnp.concatenate([att_p, att_s.reshape(n_dec, d_attn), jnp.zeros((rows - n_rows, d_attn), F32)])
        mixed = _rms_pair_cast(att, sg, g_out_a[layer][None], g_out_b[layer][None])
        h = _mm_residual(mixed, w_out, layer, h, "out_proj")

        outs["kp"].append(qkv_f[1, :n_prompt].reshape(batch, seq, n_heads, HEAD_DIM))
        outs["vp"].append(qkv_f[2, :n_prompt].reshape(batch, seq, n_heads, HEAD_DIM))
        outs["lp"].append(logf[:n_prompt].reshape(batch, seq, n_heads))
        outs["ks"].append(qkv_f[1, dec].reshape(n_dec, 1, n_heads, HEAD_DIM))
        outs["vs"].append(qkv_f[2, dec].reshape(n_dec, 1, n_heads, HEAD_DIM))
        outs["ls"].append(logf[dec].reshape(n_dec, 1, n_heads))
        outs["us"].append(uvs[1, dec].reshape(n_dec, 1, d_sgu))

        if layer % 2 == 0:
            c = _rms_cast(h, g_ffn[layer][None])
            hid = _ffn_up(c, w_dense_gate, w_dense_up, layer // 2)
            h = _mm_residual(hid, w_dense_down, layer // 2, h, "ffn_down")
            a_pe = _rms_cast(h, g_pe[layer][None])
        else:
            h, a_pe = _moe_ffn(h, g_ffn[layer][None], w_router, b_router, w_moe_gate, w_moe_up, w_moe_down,
                               layer // 2, n_rows, g_pe[layer][None])

        h = _ple(a_pe, w_pg, p_all[layer], w_pe, layer, h)
        if layer + 1 < depth:
            a = _rms_cast(h, g_mix[layer + 1][None])

    y_prompt = h[:n_prompt].reshape(batch, seq, d_model)
    y_sample = h[n_prompt:n_rows].reshape(n_dec, 1, d_model)
    return (y_prompt, y_sample, jnp.stack(outs["kp"]), jnp.stack(outs["vp"]), jnp.stack(outs["lp"]),
            jnp.stack(outs["ks"]), jnp.stack(outs["vs"]), jnp.stack(outs["ls"]), jnp.stack(outs["us"]))
```

```python
import functools

import jax
import jax.numpy as jnp
from jax import lax
from jax.experimental import pallas as pl
from jax.experimental.pallas import tpu as pltpu

F32 = jnp.float32
BF16 = jnp.bfloat16
I32 = jnp.int32

V7X_LANES = 128
V7X_SUBLANES = 8
V7X_VMEM_BYTES = 64 * 1024 * 1024
VMEM_LIMIT_BYTES = V7X_VMEM_BYTES - 8 * 1024 * 1024

HEAD_DIM = 128
GROUP_DIM = 128
CHUNK = 128
N_EXPERTS = 8
EPS = 1e-6
ATTN_SCALE = HEAD_DIM ** -0.5

ROW_TILE = 256
MM_ROW_TARGET = 1056
EXPERT_TILE = 768
EXPERT_DOWN_SPLIT = 2
GATHER_TARGET = 512
GATHER_UNROLL = 8
PAGES_PER_STEP = 8
BF16_ROWS = 16

_NT = (((1,), (1,)), ((), ()))


def _params(*semantics):
    return pltpu.CompilerParams(dimension_semantics=semantics, vmem_limit_bytes=VMEM_LIMIT_BYTES)


def _col_tile(n, target):
    t = min(target, n)
    t -= t % V7X_LANES
    while n % t:
        t -= V7X_LANES
    return t


def _row_tile(n, target, multiple=V7X_SUBLANES):
    t = min(target, n)
    t -= t % multiple
    while n % t:
        t -= multiple
    return t


def _mm_row_tile(rows, target=MM_ROW_TARGET):
    return _row_tile(rows, target, BF16_ROWS)


def _rms(x, g):
    return x * lax.rsqrt(jnp.mean(x * x, axis=-1, keepdims=True) + EPS) * g


def _gelu(x):
    return 0.5 * x * (1.0 + jnp.tanh(0.7978845608028654 * (x + 0.044715 * (x * x * x))))


def _sigmoid(x):
    return 1.0 / (1.0 + jnp.exp(-x))


def _log_sigmoid(x):
    return jnp.minimum(x, 0.0) - jnp.log1p(jnp.exp(-jnp.abs(x)))


def _split3(x):
    x1 = x.astype(BF16)
    r1 = x - x1.astype(F32)
    x2 = r1.astype(BF16)
    x3 = (r1 - x2.astype(F32)).astype(BF16)
    return x1, x2, x3


def _bf16_round(x):
    return x.astype(BF16).astype(F32)


def _rms_kernel(x_ref, g_ref, o_ref):
    o_ref[...] = _rms(x_ref[...], g_ref[...]).astype(o_ref.dtype)


def _rms_cast(x, g):
    rows, d = x.shape
    return pl.pallas_call(
        _rms_kernel,
        grid=(pl.cdiv(rows, ROW_TILE),),
        in_specs=[pl.BlockSpec((ROW_TILE, d), lambda i: (i, 0)),
                  pl.BlockSpec((1, d), lambda i: (0, 0))],
        out_specs=pl.BlockSpec((ROW_TILE, d), lambda i: (i, 0)),
        out_shape=jax.ShapeDtypeStruct((rows, d), BF16),
        compiler_params=_params("arbitrary"),
        name="rms_cast",
    )(x, g)


def _rms_pair_kernel(a_ref, b_ref, ga_ref, gb_ref, o_ref):
    da = a_ref.shape[1]
    o_ref[:, :da] = _rms(a_ref[...], ga_ref[...]).astype(o_ref.dtype)
    o_ref[:, da:] = _rms(b_ref[...], gb_ref[...]).astype(o_ref.dtype)


def _rms_pair_cast(a, b, ga, gb):
    rows, da = a.shape
    db = b.shape[1]
    return pl.pallas_call(
        _rms_pair_kernel,
        grid=(pl.cdiv(rows, ROW_TILE),),
        in_specs=[pl.BlockSpec((ROW_TILE, da), lambda i: (i, 0)),
                  pl.BlockSpec((ROW_TILE, db), lambda i: (i, 0)),
                  pl.BlockSpec((1, da), lambda i: (0, 0)),
                  pl.BlockSpec((1, db), lambda i: (0, 0))],
        out_specs=pl.BlockSpec((ROW_TILE, da + db), lambda i: (i, 0)),
        out_shape=jax.ShapeDtypeStruct((rows, da + db), BF16),
        compiler_params=_params("arbitrary"),
        name="rms_pair_cast",
    )(a, b, ga, gb)


def _cast_weight_once(w_ref, wb_ref):
    @pl.when(pl.program_id(1) == 0)
    def _():
        wb_ref[...] = w_ref[...].astype(BF16)


def _qkv_kernel(a_ref, w_ref, g_ref, of_ref, ob_ref, wb_ref):
    _cast_weight_once(w_ref, wb_ref)
    z = jnp.dot(a_ref[...], wb_ref[...], preferred_element_type=F32)
    j = pl.program_id(0)

    @pl.when(j < 2)
    def _():
        g = g_ref[...]
        for h in range(z.shape[1] // HEAD_DIM):
            sl = slice(h * HEAD_DIM, (h + 1) * HEAD_DIM)
            y = _rms(z[:, sl], g)
            of_ref[:, sl] = y
            ob_ref[:, sl] = y.astype(BF16)

    @pl.when(j == 2)
    def _():
        of_ref[...] = z
        ob_ref[...] = z.astype(BF16)


def _qkv_proj(a, w_in, layer, gains, d_attn):
    rows, d = a.shape
    tm = _mm_row_tile(rows)
    out = jax.ShapeDtypeStruct((3, rows, d_attn), F32), jax.ShapeDtypeStruct((3, rows, d_attn), BF16)
    return pl.pallas_call(
        _qkv_kernel,
        grid=(3, rows // tm),
        in_specs=[pl.BlockSpec((tm, d), lambda j, i: (i, 0)),
                  pl.BlockSpec((None, d, d_attn), lambda j, i: (layer, 0, j)),
                  pl.BlockSpec((None, 1, HEAD_DIM), lambda j, i: (j, 0, 0))],
        out_specs=[pl.BlockSpec((None, tm, d_attn), lambda j, i: (j, i, 0)),
                   pl.BlockSpec((None, tm, d_attn), lambda j, i: (j, i, 0))],
        out_shape=out,
        scratch_shapes=[pltpu.VMEM((d, d_attn), BF16)],
        compiler_params=_params("arbitrary", "arbitrary"),
        name="qkv_proj",
    )(a, w_in, gains)


def _uvs_kernel(a_ref, w_ref, g_ref, o_ref, wb_ref):
    _cast_weight_once(w_ref, wb_ref)
    x = _gelu(jnp.dot(a_ref[...], wb_ref[...], preferred_element_type=F32))
    j = pl.program_id(0)

    @pl.when(j == 0)
    def _():
        o_ref[...] = x

    @pl.when(j == 1)
    def _():
        o_ref[...] = _rms(x, g_ref[...])


def _uvs_proj(a, w_uvs, layer, g_sgu):
    rows, d = a.shape
    d_sgu = w_uvs.shape[2] // 2
    tm = _mm_row_tile(rows)
    return pl.pallas_call(
        _uvs_kernel,
        grid=(2, rows // tm),
        in_specs=[pl.BlockSpec((tm, d), lambda j, i: (i, 0)),
                  pl.BlockSpec((None, d, d_sgu), lambda j, i: (layer, 0, j)),
                  pl.BlockSpec((1, d_sgu), lambda j, i: (0, 0))],
        out_specs=pl.BlockSpec((None, tm, d_sgu), lambda j, i: (j, i, 0)),
        out_shape=jax.ShapeDtypeStruct((2, rows, d_sgu), F32),
        scratch_shapes=[pltpu.VMEM((d, d_sgu), BF16)],
        compiler_params=_params("arbitrary", "arbitrary"),
        name="uvs_proj",
    )(a, w_uvs, g_sgu)


def _mm_res_kernel(a_ref, w_ref, r_ref, o_ref, wb_ref):
    _cast_weight_once(w_ref, wb_ref)
    o_ref[...] = r_ref[...] + jnp.dot(a_ref[...], wb_ref[...], preferred_element_type=F32)


def _mm_residual(a, w, layer, res, name, row_target, col_target):
    rows, k = a.shape
    n = w.shape[2]
    tm = _mm_row_tile(rows, row_target)
    tn = _col_tile(n, col_target)
    return pl.pallas_call(
        _mm_res_kernel,
        grid=(n // tn, rows // tm),
        in_specs=[pl.BlockSpec((tm, k), lambda j, i: (i, 0)),
                  pl.BlockSpec((None, k, tn), lambda j, i: (layer, 0, j)),
                  pl.BlockSpec((tm, tn), lambda j, i: (i, j))],
        out_specs=pl.BlockSpec((tm, tn), lambda j, i: (i, j)),
        out_shape=jax.ShapeDtypeStruct((rows, n), F32),
        scratch_shapes=[pltpu.VMEM((k, tn), BF16)],
        compiler_params=_params("arbitrary", "arbitrary"),
        name=name,
    )(a, w, res)


def _ffn_up_kernel(a_ref, wg_ref, wu_ref, o_ref, wgb_ref, wub_ref):
    _cast_weight_once(wg_ref, wgb_ref)
    _cast_weight_once(wu_ref, wub_ref)
    a = a_ref[...]
    g = jnp.dot(a, wgb_ref[...], preferred_element_type=F32)
    u = jnp.dot(a, wub_ref[...], preferred_element_type=F32)
    o_ref[...] = (g * _sigmoid(g) * u).astype(o_ref.dtype)


def _ffn_up(a, wg, wu, layer):
    rows, k = a.shape
    n = wg.shape[2]
    tm = _mm_row_tile(rows)
    tn = _col_tile(n, 512)
    wspec = pl.BlockSpec((None, k, tn), lambda j, i: (layer, 0, j))
    return pl.pallas_call(
        _ffn_up_kernel,
        grid=(n // tn, rows // tm),
        in_specs=[pl.BlockSpec((tm, k), lambda j, i: (i, 0)), wspec, wspec],
        out_specs=pl.BlockSpec((tm, tn), lambda j, i: (i, j)),
        out_shape=jax.ShapeDtypeStruct((rows, n), BF16),
        scratch_shapes=[pltpu.VMEM((k, tn), BF16), pltpu.VMEM((k, tn), BF16)],
        compiler_params=_params("arbitrary", "arbitrary"),
        name="ffn_up",
    )(a, wg, wu)


def _ple_kernel(a_ref, w_ref, p_ref, wpe_ref, r_ref, o_ref, wb_ref):
    _cast_weight_once(w_ref, wb_ref)
    gate = _sigmoid(jnp.dot(a_ref[...], wb_ref[...], preferred_element_type=F32))
    emb = jnp.dot(p_ref[...].astype(BF16), wpe_ref[...].astype(BF16), preferred_element_type=F32)
    o_ref[...] = r_ref[...] + emb * gate


def _ple(a, w_pg, p, w_pe, layer, res):
    rows, k = a.shape
    n = w_pg.shape[2]
    d_ple = p.shape[1]
    tm = _mm_row_tile(rows)
    tn = _col_tile(n, 1024)
    return pl.pallas_call(
        _ple_kernel,
        grid=(n // tn, rows // tm),
        in_specs=[pl.BlockSpec((tm, k), lambda j, i: (i, 0)),
                  pl.BlockSpec((None, k, tn), lambda j, i: (layer, 0, j)),
                  pl.BlockSpec((tm, d_ple), lambda j, i: (i, 0)),
                  pl.BlockSpec((None, d_ple, tn), lambda j, i: (layer, 0, j)),
                  pl.BlockSpec((tm, tn), lambda j, i: (i, j))],
        out_specs=pl.BlockSpec((tm, tn), lambda j, i: (i, j)),
        out_shape=jax.ShapeDtypeStruct((rows, n), F32),
        scratch_shapes=[pltpu.VMEM((k, tn), BF16)],
        compiler_params=_params("arbitrary", "arbitrary"),
        name="ple",
    )(a, w_pg, p, w_pe, res)


def _forget_kernel(a_ref, wf_ref, wft_ref, bf_ref, bft_ref, lf_ref, cr_ref, carr_ref, *, seq):
    i = pl.program_id(0)
    tm = a_ref.shape[0]
    n_h = lf_ref.shape[1]
    a = a_ref[...]
    lf = _log_sigmoid(jnp.dot(a, wf_ref[...].astype(BF16), preferred_element_type=F32) + bf_ref[...])
    lft = _log_sigmoid(lax.dot_general(wft_ref[...].astype(BF16), a, _NT, preferred_element_type=F32)
                       + bft_ref[...])

    @pl.when((i * tm) % seq == 0)
    def _():
        carr_ref[...] = jnp.zeros_like(carr_ref)

    upto = (lax.broadcasted_iota(I32, (tm, tm), 0) <= lax.broadcasted_iota(I32, (tm, tm), 1)).astype(BF16)
    ct = carr_ref[...]
    for piece in _split3(lft):
        ct = ct + jnp.dot(piece, upto, preferred_element_type=F32)
    carr_ref[...] = ct[:, tm - 1:tm]
    lf_ref[...] = lf[:, :n_h]
    cr_ref[...] = ct


def _forget_gates(a, wf_pad, wf_t, bf_pad, bf_t, seq):
    rows, d = a.shape
    n_h = wf_t.shape[0]
    n_tiles = pl.cdiv(rows, ROW_TILE)
    return pl.pallas_call(
        functools.partial(_forget_kernel, seq=seq),
        grid=(n_tiles,),
        in_specs=[pl.BlockSpec((ROW_TILE, d), lambda i: (i, 0)),
                  pl.BlockSpec((d, V7X_LANES), lambda i: (0, 0)),
                  pl.BlockSpec((n_h, d), lambda i: (0, 0)),
                  pl.BlockSpec((1, V7X_LANES), lambda i: (0, 0)),
                  pl.BlockSpec((n_h, 1), lambda i: (0, 0))],
        out_specs=[pl.BlockSpec((ROW_TILE, n_h), lambda i: (i, 0)),
                   pl.BlockSpec((None, n_h, ROW_TILE), lambda i: (i, 0, 0))],
        out_shape=(jax.ShapeDtypeStruct((rows, n_h), F32),
                   jax.ShapeDtypeStruct((n_tiles, n_h, ROW_TILE), F32)),
        scratch_shapes=[pltpu.VMEM((n_h, 1), F32)],
        compiler_params=_params("arbitrary"),
        name="forget_gates",
    )(a, wf_pad, wf_t, bf_pad, bf_t)


def _sgu_kernel(u_ref, v_ref, w_ref, b_ref, o_ref, *, n_prompt_chunks):
    c = pl.program_id(0)
    n_groups = w_ref.shape[0]
    row = lax.broadcasted_iota(I32, (CHUNK, CHUNK), 0)
    col = lax.broadcasted_iota(I32, (CHUNK, CHUNK), 1)

    @pl.when(c < n_prompt_chunks)
    def _():
        for g in range(n_groups):
            sl = slice(g * GROUP_DIM, (g + 1) * GROUP_DIM)
            wm = jnp.where(col <= row, w_ref[g], 0.0).astype(BF16)
            z = jnp.dot(wm, v_ref[:, sl].astype(BF16), preferred_element_type=F32) + b_ref[:, g:g + 1]
            o_ref[:, sl] = u_ref[:, sl] * z

    @pl.when(c >= n_prompt_chunks)
    def _():
        for g in range(n_groups):
            sl = slice(g * GROUP_DIM, (g + 1) * GROUP_DIM)
            w00 = _bf16_round(w_ref[g][0:1, 0:1])
            z = w00 * _bf16_round(v_ref[:, sl]) + b_ref[0:1, g:g + 1]
            o_ref[:, sl] = u_ref[:, sl] * z


def _sgu(uvs, w_s, b_s_t, layer, n_prompt_rows):
    _, rows, d_sgu = uvs.shape
    n_groups = w_s.shape[1]
    return pl.pallas_call(
        functools.partial(_sgu_kernel, n_prompt_chunks=n_prompt_rows // CHUNK),
        grid=(rows // CHUNK,),
        in_specs=[pl.BlockSpec((None, CHUNK, d_sgu), lambda c: (0, c, 0)),
                  pl.BlockSpec((None, CHUNK, d_sgu), lambda c: (1, c, 0)),
                  pl.BlockSpec((None, n_groups, CHUNK, CHUNK), lambda c: (layer, 0, 0, 0)),
                  pl.BlockSpec((None, CHUNK, n_groups), lambda c: (layer, 0, 0))],
        out_specs=pl.BlockSpec((CHUNK, d_sgu), lambda c: (c, 0)),
        out_shape=jax.ShapeDtypeStruct((rows, d_sgu), F32),
        compiler_params=_params("arbitrary"),
        name="sgu",
    )(uvs, uvs, w_s, b_s_t)


def _attn_kernel(q_ref, k_ref, v_ref, cr_ref, o_ref, m_ref, l_ref):
    qi = pl.program_id(1)
    ts = q_ref.shape[0]
    n_heads = q_ref.shape[1] // HEAD_DIM
    row = lax.broadcasted_iota(I32, (ts, ts), 0)
    col = lax.broadcasted_iota(I32, (ts, ts), 1)
    m_ref[...] = jnp.full_like(m_ref, -jnp.inf)
    l_ref[...] = jnp.zeros_like(l_ref)
    o_ref[...] = jnp.zeros_like(o_ref)

    def key_tile(kt, masked):
        off = pl.multiple_of(kt * ts, ts)
        for h in range(n_heads):
            sl = slice(h * HEAD_DIM, (h + 1) * HEAD_DIM)
            s = lax.dot_general(q_ref[:, sl], k_ref[pl.ds(off, ts), sl], _NT, preferred_element_type=F32)
            s = s - cr_ref[kt, h:h + 1, :]
            if masked:
                s = jnp.where(col <= row, s, -jnp.inf)
            m = m_ref[h]
            m_new = jnp.maximum(m, jnp.max(s, axis=-1, keepdims=True))
            alpha = jnp.exp(m - m_new)
            p = jnp.exp(s - m_new)
            l_ref[h] = alpha * l_ref[h] + jnp.sum(p, axis=-1, keepdims=True)
            o_ref[:, sl] = alpha * o_ref[:, sl] + jnp.dot(p.astype(BF16), v_ref[pl.ds(off, ts), sl],
                                                          preferred_element_type=F32)
            m_ref[h] = m_new

    def body(kt, carry):
        key_tile(kt, False)
        return carry

    lax.fori_loop(0, qi, body, 0)
    key_tile(qi, True)
    for h in range(n_heads):
        sl = slice(h * HEAD_DIM, (h + 1) * HEAD_DIM)
        o_ref[:, sl] = o_ref[:, sl] / l_ref[h]


def _prompt_attention(qkv_b, c_row, batch, seq):
    d_attn = qkv_b.shape[2]
    n_heads = d_attn // HEAD_DIM
    ts = ROW_TILE
    nq = seq // ts
    return pl.pallas_call(
        _attn_kernel,
        grid=(batch, nq),
        in_specs=[pl.BlockSpec((None, ts, d_attn), lambda b, qi: (0, b * nq + qi, 0)),
                  pl.BlockSpec((None, seq, d_attn), lambda b, qi: (1, b, 0)),
                  pl.BlockSpec((None, seq, d_attn), lambda b, qi: (2, b, 0)),
                  pl.BlockSpec((nq, n_heads, ts), lambda b, qi: (b, 0, 0))],
        out_specs=pl.BlockSpec((ts, d_attn), lambda b, qi: (b * nq + qi, 0)),
        out_shape=jax.ShapeDtypeStruct((batch * seq, d_attn), F32),
        scratch_shapes=[pltpu.VMEM((n_heads, ts, 1), F32), pltpu.VMEM((n_heads, ts, 1), F32)],
        compiler_params=_params("arbitrary", "arbitrary"),
        name="prompt_attention",
    )(qkv_b, qkv_b, qkv_b, c_row)


def _page_suffix_kernel(lf_ref, o_ref):
    page = lf_ref.shape[1]
    later = (lax.broadcasted_iota(I32, (page, page), 0) > lax.broadcasted_iota(I32, (page, page), 1)).astype(BF16)
    acc = jnp.zeros(o_ref.shape, F32)
    for piece in _split3(lf_ref[...]):
        acc = acc + jnp.dot(piece, later, preferred_element_type=F32)
    o_ref[...] = acc


def _page_suffix(lf_t):
    depth, rows, page = lf_t.shape
    tile = _row_tile(rows, 1024)
    spec = pl.BlockSpec((None, tile, page), lambda d, i: (d, i, 0))
    return pl.pallas_call(
        _page_suffix_kernel,
        grid=(depth, rows // tile),
        in_specs=[spec],
        out_specs=spec,
        out_shape=jax.ShapeDtypeStruct(lf_t.shape, F32),
        compiler_params=_params("arbitrary", "arbitrary"),
        name="page_suffix",
    )(lf_t)


def _decode_kernel(pt_ref, q_ref, kn_ref, vn_ref, lfn_ref, *rest, n_pp):
    del pt_ref
    k_refs, v_refs = rest[:n_pp], rest[n_pp:2 * n_pp]
    lf_refs, suf_refs = rest[2 * n_pp:3 * n_pp], rest[3 * n_pp:4 * n_pp]
    o_ref, m_ref, l_ref, acc_ref, car_ref = rest[4 * n_pp:]
    g = pl.program_id(1)
    n_h = q_ref.shape[0]
    width = k_refs[0].shape[0]
    own_head = (lax.rem(lax.broadcasted_iota(I32, (n_h, width), 1), n_h)
                == lax.broadcasted_iota(I32, (n_h, width), 0))
    lane_is_head = (lax.broadcasted_iota(I32, (n_h, V7X_LANES), 1)
                    == lax.broadcasted_iota(I32, (n_h, V7X_LANES), 0))
    q = q_ref[...]
    qb = q.astype(BF16)

    @pl.when(g == 0)
    def _():
        m_ref[...] = jnp.full_like(m_ref, -jnp.inf)
        l_ref[...] = jnp.zeros_like(l_ref)
        acc_ref[...] = jnp.zeros_like(acc_ref)
        car_ref[...] = lfn_ref[...]

    for r in range(n_pp):
        suf = suf_refs[r][...]
        car = car_ref[...]
        s = lax.dot_general(qb, k_refs[r][...].astype(BF16), _NT, preferred_element_type=F32)
        s = jnp.where(own_head, s + (suf + car), -jnp.inf)
        m = m_ref[...]
        m_new = jnp.maximum(m, jnp.max(s, axis=-1, keepdims=True))
        alpha = jnp.exp(m - m_new)
        p = jnp.exp(s - m_new)
        l_ref[...] = alpha * l_ref[...] + jnp.sum(p, axis=-1, keepdims=True)
        acc_ref[...] = alpha * acc_ref[...] + jnp.dot(p.astype(BF16), v_refs[r][...].astype(BF16),
                                                      preferred_element_type=F32)
        m_ref[...] = m_new
        page_mass = suf[:, :V7X_LANES] + lf_refs[r][:, :V7X_LANES]
        car_ref[...] = car + jnp.sum(jnp.where(lane_is_head, page_mass, 0.0), axis=-1, keepdims=True)

    @pl.when(g == pl.num_programs(1) - 1)
    def _():
        s_self = jnp.sum(q * _bf16_round(kn_ref[...]), axis=-1, keepdims=True)
        m = m_ref[...]
        m_new = jnp.maximum(m, s_self)
        alpha = jnp.exp(m - m_new)
        p_self = jnp.exp(s_self - m_new)
        l = alpha * l_ref[...] + p_self
        acc = alpha * acc_ref[...] + _bf16_round(p_self) * _bf16_round(vn_ref[...])
        o_ref[...] = acc / l


def _decode_attention(q, k_new, v_new, lf_new, cache_k, cache_v, lf_rows, suf_rows, page_table, layer):
    n_dec, n_h, head_dim = q.shape
    n_pages = page_table.shape[1]
    width = cache_k.shape[2]
    n_pp = PAGES_PER_STEP
    while n_pages % n_pp:
        n_pp //= 2

    def page_map(r):
        return lambda b, g, pt: (layer, pt[b * n_pages + (n_pages - 1 - (g * n_pp + r))], 0, 0)

    vec = pl.BlockSpec((None, n_h, head_dim), lambda b, g, pt: (b, 0, 0))
    in_specs = [vec, vec, vec, pl.BlockSpec((None, n_h, 1), lambda b, g, pt: (b, 0, 0))]
    in_specs += [pl.BlockSpec((None, None, width, head_dim), page_map(r)) for r in range(n_pp)] * 2
    in_specs += [pl.BlockSpec((None, None, 1, width), page_map(r)) for r in range(n_pp)] * 2
    grid_spec = pltpu.PrefetchScalarGridSpec(
        num_scalar_prefetch=1,
        grid=(n_dec, n_pages // n_pp),
        in_specs=in_specs,
        out_specs=vec,
        scratch_shapes=[pltpu.VMEM((n_h, 1), F32), pltpu.VMEM((n_h, 1), F32),
                        pltpu.VMEM((n_h, head_dim), F32), pltpu.VMEM((n_h, 1), F32)],
    )
    return pl.pallas_call(
        functools.partial(_decode_kernel, n_pp=n_pp),
        grid_spec=grid_spec,
        out_shape=jax.ShapeDtypeStruct((n_dec, n_h, head_dim), F32),
        compiler_params=_params("arbitrary", "arbitrary"),
        name="decode_attention",
    )(page_table.reshape(-1), q, k_new, v_new, lf_new,
      *([cache_k] * n_pp), *([cache_v] * n_pp), *([lf_rows] * n_pp), *([suf_rows] * n_pp))


def _route_kernel(h_ref, g_ref, wr_ref, br_ref, c_ref, idx_ref, gate_ref, cnt_ref, *, n_valid_rows):
    i = pl.program_id(0)
    tm = h_ref.shape[0]
    lanes = wr_ref.shape[1]
    c = _rms(h_ref[...], g_ref[...])
    c_ref[...] = c
    logits = jnp.dot(c, wr_ref[...], preferred_element_type=F32, precision=lax.Precision.HIGHEST) + br_ref[...]
    lane = lax.broadcasted_iota(I32, (tm, lanes), 1)
    lg = jnp.where(lane < N_EXPERTS, logits, -jnp.inf)
    m1 = jnp.max(lg, axis=-1, keepdims=True)
    i1 = jnp.min(jnp.where(lg == m1, lane, lanes), axis=-1, keepdims=True)
    lg2 = jnp.where(lane == i1, -jnp.inf, lg)
    m2 = jnp.max(lg2, axis=-1, keepdims=True)
    i2 = jnp.min(jnp.where(lg2 == m2, lane, lanes), axis=-1, keepdims=True)
    e2 = jnp.exp(m2 - m1)
    valid = (i * tm + lax.broadcasted_iota(I32, (tm, 1), 0)) < n_valid_rows
    g1 = jnp.where(valid, 1.0 / (1.0 + e2), 0.0)
    g2 = jnp.where(valid, e2 / (1.0 + e2), 0.0)

    @pl.when(i == 0)
    def _():
        cnt_ref[...] = jnp.zeros_like(cnt_ref)

    sel = jnp.where(((lane == i1) | (lane == i2)) & valid, 1.0, 0.0)
    before = (lax.broadcasted_iota(I32, (tm, tm), 1) < lax.broadcasted_iota(I32, (tm, tm), 0)).astype(BF16)
    cnt = cnt_ref[...]
    rank = jnp.dot(before, sel.astype(BF16), preferred_element_type=F32) + cnt.astype(F32)
    cnt_ref[...] = cnt + jnp.sum(sel, axis=0, keepdims=True).astype(I32)
    r1 = jnp.sum(jnp.where(lane == i1, rank, 0.0), axis=-1, keepdims=True).astype(I32)
    r2 = jnp.sum(jnp.where(lane == i2, rank, 0.0), axis=-1, keepdims=True).astype(I32)
    out_lane = lax.broadcasted_iota(I32, idx_ref.shape, 1)
    idx_ref[...] = jnp.where(out_lane == 0, i1, jnp.where(out_lane == 1, i2, jnp.where(out_lane == 2, r1, r2)))
    gate_ref[...] = jnp.where(out_lane == 0, g1, jnp.where(out_lane == 1, g2, 0.0))


def _route(h, g, wr_pad, br_pad, n_valid_rows):
    rows, d = h.shape
    lanes = wr_pad.shape[1]
    return pl.pallas_call(
        functools.partial(_route_kernel, n_valid_rows=n_valid_rows),
        grid=(pl.cdiv(rows, ROW_TILE),),
        in_specs=[pl.BlockSpec((ROW_TILE, d), lambda i: (i, 0)),
                  pl.BlockSpec((1, d), lambda i: (0, 0)),
                  pl.BlockSpec((d, lanes), lambda i: (0, 0)),
                  pl.BlockSpec((1, lanes), lambda i: (0, 0))],
        out_specs=[pl.BlockSpec((ROW_TILE, d), lambda i: (i, 0)),
                   pl.BlockSpec((ROW_TILE, N_EXPERTS), lambda i: (i, 0)),
                   pl.BlockSpec((ROW_TILE, N_EXPERTS), lambda i: (i, 0)),
                   pl.BlockSpec((1, lanes), lambda i: (0, 0))],
        out_shape=(jax.ShapeDtypeStruct((rows, d), F32),
                   jax.ShapeDtypeStruct((rows, N_EXPERTS), I32),
                   jax.ShapeDtypeStruct((rows, N_EXPERTS), F32),
                   jax.ShapeDtypeStruct((1, lanes), I32)),
        compiler_params=_params("arbitrary"),
        name="route",
    )(h, g, wr_pad, br_pad)


def _row_copy(table_ref, dst_ref, row, r, sem):
    return pltpu.make_async_copy(table_ref.at[pl.ds(row, 1)], dst_ref.at[pl.ds(r, 1)], sem)


def _gather_kernel(idx_ref, n_ref, table_ref, o_ref, buf_ref, sem):
    tg = o_ref.shape[0]
    base = pl.program_id(0) * tg
    wanted = base < n_ref[0]

    @pl.when(wanted)
    def _():
        def start(r, carry):
            _row_copy(table_ref, buf_ref, idx_ref[base + r], r, sem).start()
            return carry

        def wait(r, carry):
            _row_copy(table_ref, buf_ref, 0, r, sem).wait()
            return carry

        lax.fori_loop(0, tg, start, 0, unroll=GATHER_UNROLL)
        lax.fori_loop(0, tg, wait, 0, unroll=GATHER_UNROLL)
        o_ref[...] = buf_ref[...].astype(o_ref.dtype)

    @pl.when(jnp.logical_not(wanted))
    def _():
        o_ref[...] = jnp.zeros_like(o_ref)


def _gather_rows(table, idx, n_wanted, out_dtype):
    n_out = idx.shape[0]
    d = table.shape[1]
    tg = _row_tile(n_out, GATHER_TARGET, BF16_ROWS * GATHER_UNROLL)
    grid_spec = pltpu.PrefetchScalarGridSpec(
        num_scalar_prefetch=2,
        grid=(n_out // tg,),
        in_specs=[pl.BlockSpec(memory_space=pl.ANY)],
        out_specs=pl.BlockSpec((tg, d), lambda i, idx, n: (i, 0)),
        scratch_shapes=[pltpu.VMEM((tg, d), table.dtype), pltpu.SemaphoreType.DMA(())],
    )
    return pl.pallas_call(
        _gather_kernel,
        grid_spec=grid_spec,
        out_shape=jax.ShapeDtypeStruct((n_out, d), out_dtype),
        compiler_params=_params("arbitrary"),
        name="gather_rows",
    )(idx, n_wanted, table)


def _cast_expert_weight(te_ref, w_ref, wb_ref):
    i = pl.program_id(1)

    @pl.when((i == 0) | (te_ref[i] != te_ref[jnp.maximum(i - 1, 0)]))
    def _():
        wb_ref[...] = w_ref[...].astype(BF16)


def _moe_up_kernel(te_ref, nu_ref, x_ref, wg_ref, wu_ref, o_ref, wgb_ref, wub_ref):
    _cast_expert_weight(te_ref, wg_ref, wgb_ref)
    _cast_expert_weight(te_ref, wu_ref, wub_ref)
    used = pl.program_id(1) < nu_ref[0]

    @pl.when(used)
    def _():
        x = x_ref[...]
        g = jnp.dot(x, wgb_ref[...], preferred_element_type=F32)
        u = jnp.dot(x, wub_ref[...], preferred_element_type=F32)
        o_ref[...] = (g * _sigmoid(g) * u).astype(o_ref.dtype)

    @pl.when(jnp.logical_not(used))
    def _():
        o_ref[...] = jnp.zeros_like(o_ref)


def _moe_up(xs, wg, wu, layer, tile_expert, n_used):
    rows, k = xs.shape
    n = wg.shape[3]
    tn = _col_tile(n, 1024)
    wspec = pl.BlockSpec((None, None, k, tn), lambda j, i, te, nu: (layer, te[i], 0, j))
    grid_spec = pltpu.PrefetchScalarGridSpec(
        num_scalar_prefetch=2,
        grid=(n // tn, rows // EXPERT_TILE),
        in_specs=[pl.BlockSpec((EXPERT_TILE, k), lambda j, i, te, nu: (jnp.minimum(i, nu[0] - 1), 0)),
                  wspec, wspec],
        out_specs=pl.BlockSpec((EXPERT_TILE, tn), lambda j, i, te, nu: (i, j)),
        scratch_shapes=[pltpu.VMEM((k, tn), BF16), pltpu.VMEM((k, tn), BF16)],
    )
    return pl.pallas_call(
        _moe_up_kernel,
        grid_spec=grid_spec,
        out_shape=jax.ShapeDtypeStruct((rows, n), BF16),
        compiler_params=_params("arbitrary", "arbitrary"),
        name="moe_up",
    )(tile_expert, n_used, xs, wg, wu)


def _moe_down_kernel(te_ref, nu_ref, x_ref, w_ref, o_ref, wb_ref):
    _cast_expert_weight(te_ref, w_ref, wb_ref)
    used = pl.program_id(1) < nu_ref[0]

    @pl.when(used)
    def _():
        o_ref[...] = jnp.dot(x_ref[...], wb_ref[...], preferred_element_type=F32)

    @pl.when(jnp.logical_not(used))
    def _():
        o_ref[...] = jnp.zeros_like(o_ref)


def _moe_down(hs, wd, layer, tile_expert, n_used):
    rows, k = hs.shape
    n = wd.shape[3]
    tn = _col_tile(n, 512)
    tm = EXPERT_TILE // EXPERT_DOWN_SPLIT
    grid_spec = pltpu.PrefetchScalarGridSpec(
        num_scalar_prefetch=2,
        grid=(n // tn, rows // tm),
        in_specs=[pl.BlockSpec((tm, k), lambda j, i, te, nu: (jnp.minimum(i, nu[0] - 1), 0)),
                  pl.BlockSpec((None, None, k, tn), lambda j, i, te, nu: (layer, te[i], 0, j))],
        out_specs=pl.BlockSpec((tm, tn), lambda j, i, te, nu: (i, j)),
        scratch_shapes=[pltpu.VMEM((k, tn), BF16)],
    )
    return pl.pallas_call(
        _moe_down_kernel,
        grid_spec=grid_spec,
        out_shape=jax.ShapeDtypeStruct((rows, n), F32),
        compiler_params=_params("arbitrary", "arbitrary"),
        name="moe_down",
    )(tile_expert, n_used, hs, wd)


def _combine_kernel(h_ref, y1_ref, y2_ref, gate_ref, g_ref, oh_ref, on_ref):
    gate = gate_ref[...]
    h = h_ref[...] + gate[:, 0:1] * y1_ref[...] + gate[:, 1:2] * y2_ref[...]
    oh_ref[...] = h
    on_ref[...] = _rms(h, g_ref[...]).astype(on_ref.dtype)


def _combine(h, y_pairs, gates, g):
    rows, d = h.shape
    return pl.pallas_call(
        _combine_kernel,
        grid=(pl.cdiv(rows, ROW_TILE),),
        in_specs=[pl.BlockSpec((ROW_TILE, d), lambda i: (i, 0)),
                  pl.BlockSpec((None, ROW_TILE, d), lambda i: (0, i, 0)),
                  pl.BlockSpec((None, ROW_TILE, d), lambda i: (1, i, 0)),
                  pl.BlockSpec((ROW_TILE, N_EXPERTS), lambda i: (i, 0)),
                  pl.BlockSpec((1, d), lambda i: (0, 0))],
        out_specs=[pl.BlockSpec((ROW_TILE, d), lambda i: (i, 0)),
                   pl.BlockSpec((ROW_TILE, d), lambda i: (i, 0))],
        out_shape=(jax.ShapeDtypeStruct((rows, d), F32), jax.ShapeDtypeStruct((rows, d), BF16)),
        compiler_params=_params("arbitrary"),
        name="moe_combine",
    )(h, y_pairs, y_pairs, gates, g)


def _moe_ffn(h, g_ffn, w_router, b_router, wg, wu, wd, layer, n_valid_rows, g_next):
    rows, d = h.shape
    wr_pad = jnp.pad(w_router[layer], ((0, 0), (0, V7X_LANES - N_EXPERTS)))
    br_pad = jnp.pad(b_router[layer], (0, V7X_LANES - N_EXPERTS))[None]
    c, idx, gates, counts = _route(h, g_ffn, wr_pad, br_pad, n_valid_rows)

    n_tiles_max = pl.cdiv(2 * n_valid_rows, EXPERT_TILE) + N_EXPERTS
    n_slots = n_tiles_max * EXPERT_TILE
    counts = counts[0, :N_EXPERTS]
    tiles = (counts + EXPERT_TILE - 1) // EXPERT_TILE
    tile_end = jnp.cumsum(tiles)
    tile_start = tile_end - tiles
    n_used = tile_end[-1:].astype(I32)
    tile_ids = jnp.minimum(jnp.arange(n_tiles_max, dtype=I32), n_used[0] - 1)
    tile_expert = jnp.sum(tile_ids[:, None] >= tile_end[None, :], axis=1).astype(I32)
    valid =(jnp.arange(rows) < n_valid_rows)[:, None]
    slot = tile_start[idx[:, 0:2]] * EXPERT_TILE + idx[:, 2:4]
    token = jnp.broadcast_to(jnp.arange(rows, dtype=I32)[:, None], slot.shape)
    src = (jnp.arange(n_slots, dtype=I32) % rows).at[jnp.where(valid, slot, n_slots).reshape(-1)].set(
        token.reshape(-1), mode="drop")

    xs = _gather_rows(c, src, n_used * EXPERT_TILE, BF16)
    hid = _moe_up(xs, wg, wu, layer, tile_expert, n_used)
    y = _moe_down(hid, wd, layer, jnp.repeat(tile_expert, EXPERT_DOWN_SPLIT),
                  n_used * EXPERT_DOWN_SPLIT)
    back = jnp.where(valid, slot, token % n_slots).astype(I32).T.reshape(-1)
    y_pairs = _gather_rows(y, back, jnp.full((1,), 2 * rows, I32), F32).reshape(2, rows, d)
    return _combine(h, y_pairs, gates, g_next)


def kernel(x_prompt, x_sample, cache_k, cache_v, cache_logf, page_table, p_prompt, p_sample,
           g_mix, w_in, b_f, g_q, g_k, w_s, b_s, g_sgu, g_out_a, g_out_b, w_out, g_ffn,
           w_dense_gate, w_dense_up, w_dense_down, w_router, b_router, w_moe_gate, w_moe_up,
           w_moe_down, g_pe, w_pe, w_pg):
    batch, seq, d_model = x_prompt.shape
    n_dec, dec_seq, _ = x_sample.shape
    assert dec_seq == 1, "the decode attention handles one new token per sequence"
    depth = w_in.shape[0]
    n_pool, page = cache_k.shape[1], cache_k.shape[2]
    n_heads = cache_k.shape[3]
    d_attn = n_heads * HEAD_DIM
    d_sgu = d_model - d_attn
    n_prompt = batch * seq
    n_rows = n_prompt + n_dec
    assert seq % ROW_TILE == 0 and n_prompt % CHUNK == 0
    rows = pl.cdiv(n_rows + CHUNK, ROW_TILE) * ROW_TILE

    h = jnp.concatenate([x_prompt.reshape(n_prompt, d_model), x_sample.reshape(n_dec, d_model),
                         jnp.zeros((rows - n_rows, d_model), F32)])
    p_all = jnp.concatenate([p_prompt.reshape(depth, n_prompt, -1), p_sample.reshape(depth, n_dec, -1),
                             jnp.zeros((depth, rows - n_rows, p_prompt.shape[-1]), F32)], axis=1)

    w_f =w_in[:, :, 3 * d_attn:3 * d_attn + n_heads]
    w_uvs = w_in[:, :, 3 * d_attn + n_heads:]
    wf_pad = jnp.pad(w_f, ((0, 0), (0, 0), (0, V7X_LANES - n_heads)))
    wf_t = jnp.swapaxes(w_f, 1, 2)
    bf_pad = jnp.pad(b_f, ((0, 0), (0, V7X_LANES - n_heads)))
    ones = jnp.ones((depth, HEAD_DIM), F32)
    qkv_gain = jnp.stack([g_q * ATTN_SCALE, g_k, ones], axis=1)[:, :, None, :]
    b_s_t = jnp.swapaxes(b_s, 1, 2)
    cache_k2 = cache_k.reshape(depth, n_pool, page * n_heads, HEAD_DIM)
    cache_v2 = cache_v.reshape(depth, n_pool, page * n_heads, HEAD_DIM)
    lf_rows = cache_logf.reshape(depth, n_pool, 1, page * n_heads)
    suf_t = _page_suffix(jnp.swapaxes(cache_logf, 2, 3).reshape(depth, n_pool * n_heads, page))
    suf_rows = jnp.swapaxes(suf_t.reshape(depth, n_pool, n_heads, page), 2, 3).reshape(depth, n_pool, 1, page * n_heads)

    outs = {name: [] for name in ("kp", "vp", "lp", "ks", "vs", "ls", "us")}
    a = _rms_cast(h, g_mix[0][None])
    for layer in range(depth):
        qkv_f, qkv_b = _qkv_proj(a, w_in, layer, qkv_gain[layer], d_attn)
        logf, c_row = _forget_gates(a, wf_pad[layer], wf_t[layer], bf_pad[layer][None], b_f[layer][:, None], seq)
        uvs = _uvs_proj(a, w_uvs, layer, g_sgu[layer][None])
        sg = _sgu(uvs, w_s, b_s_t, layer, n_prompt)
        att_p = _prompt_attention(qkv_b, c_row, batch, seq)
        dec = slice(n_prompt, n_rows)
        per_head = (n_dec, n_heads, HEAD_DIM)
        att_s = _decode_attention(qkv_b[0, dec].astype(F32).reshape(per_head), qkv_f[1, dec].reshape(per_head),
                                  qkv_f[2, dec].reshape(per_head), logf[dec][:, :, None],
                                  cache_k2, cache_v2, lf_rows, suf_rows, page_table, layer)
        att = jnp.concatenate([att_p, att_s.reshape(n_dec, d_attn), jnp.zeros((rows - n_rows, d_attn), F32)])
        mixed = _rms_pair_cast(att, sg, g_out_a[layer][None], g_out_b[layer][None])
        h = _mm_residual(mixed, w_out, layer, h, "out_proj", MM_ROW_TARGET, 1024)

        outs["kp"].append(qkv_f[1, :n_prompt].reshape(batch, seq, n_heads, HEAD_DIM))
        outs["vp"].append(qkv_f[2, :n_prompt].reshape(batch, seq, n_heads, HEAD_DIM))
        outs["lp"].append(logf[:n_prompt].reshape(batch, seq, n_heads))
        outs["ks"].append(qkv_f[1, dec].reshape(n_dec, 1, n_heads, HEAD_DIM))
        outs["vs"].append(qkv_f[2, dec].reshape(n_dec, 1, n_heads, HEAD_DIM))
        outs["ls"].append(logf[dec].reshape(n_dec, 1, n_heads))
        outs["us"].append(uvs[1, dec].reshape(n_dec, 1, d_sgu))

        if layer % 2 == 0:
            c = _rms_cast(h, g_ffn[layer][None])
            hid = _ffn_up(c, w_dense_gate, w_dense_up, layer // 2)
            h = _mm_residual(hid, w_dense_down, layer // 2, h, "ffn_down", MM_ROW_TARGET // 2, 512)
            a_pe = _rms_cast(h, g_pe[layer][None])
        else:
            h, a_pe = _moe_ffn(h, g_ffn[layer][None], w_router, b_router, w_moe_gate, w_moe_up, w_moe_down,
                               layer // 2, n_rows, g_pe[layer][None])

        h = _ple(a_pe, w_pg, p_all[layer], w_pe, layer, h)
        if layer + 1 < depth:
            a = _rms_cast(h, g_mix[layer + 1][None])

    y_prompt = h[:n_prompt].reshape(batch, seq, d_model)
    y_sample = h[n_prompt:n_rows].reshape(n_dec, 1, d_model)
    return (y_prompt, y_sample, jnp.stack(outs["kp"]), jnp.stack(outs["vp"]), jnp.stack(outs["lp"]),
            jnp.stack(outs["ks"]), jnp.stack(outs["vs"]), jnp.stack(outs["ls"]), jnp.stack(outs["us"]))
```

```python
import functools

import jax
import jax.numpy as jnp
from jax import lax
from jax.experimental import pallas as pl
from jax.experimental.pallas import tpu as pltpu

F32 = jnp.float32
BF16 = jnp.bfloat16
I32 = jnp.int32

V7X_LANES = 128
V7X_SUBLANES = 8
V7X_VMEM_BYTES = 64 * 1024 * 1024
VMEM_LIMIT_BYTES = V7X_VMEM_BYTES - 8 * 1024 * 1024

HEAD_DIM = 128
GROUP_DIM = 128
CHUNK = 128
N_EXPERTS = 8
EPS = 1e-6
ATTN_SCALE = HEAD_DIM ** -0.5

ROW_TILE = 256
MM_ROW_TARGET = 1056
EXPERT_TILE = 768
EXPERT_DOWN_SPLIT = 2
GATHER_TARGET = 512
GATHER_UNROLL = 8
PAGES_PER_STEP = 8
BF16_ROWS = 16

_NT = (((1,), (1,)), ((), ()))


def _params(*semantics):
    return pltpu.CompilerParams(dimension_semantics=semantics, vmem_limit_bytes=VMEM_LIMIT_BYTES)


def _col_tile(n, target):
    t = min(target, n)
    t -= t % V7X_LANES
    while n % t:
        t -= V7X_LANES
    return t


def _row_tile(n, target, multiple=V7X_SUBLANES):
    t = min(target, n)
    t -= t % multiple
    while n % t:
        t -= multiple
    return t


def _mm_row_tile(rows, target=MM_ROW_TARGET):
    return _row_tile(rows, target, BF16_ROWS)


def _rms(x, g):
    return x * lax.rsqrt(jnp.mean(x * x, axis=-1, keepdims=True) + EPS) * g


def _gelu(x):
    return 0.5 * x * (1.0 + jnp.tanh(0.7978845608028654 * (x + 0.044715 * (x * x * x))))


def _sigmoid(x):
    return 1.0 / (1.0 + jnp.exp(-x))


def _log_sigmoid(x):
    return jnp.minimum(x, 0.0) - jnp.log1p(jnp.exp(-jnp.abs(x)))


def _split3(x):
    x1 = x.astype(BF16)
    r1 = x - x1.astype(F32)
    x2 = r1.astype(BF16)
    x3 = (r1 - x2.astype(F32)).astype(BF16)
    return x1, x2, x3


def _bf16_round(x):
    return x.astype(BF16).astype(F32)


def _rms_kernel(x_ref, g_ref, o_ref):
    o_ref[...] = _rms(x_ref[...], g_ref[...]).astype(o_ref.dtype)


def _rms_cast(x, g):
    rows, d = x.shape
    return pl.pallas_call(
        _rms_kernel,
        grid=(pl.cdiv(rows, ROW_TILE),),
        in_specs=[pl.BlockSpec((ROW_TILE, d), lambda i: (i, 0)),
                  pl.BlockSpec((1, d), lambda i: (0, 0))],
        out_specs=pl.BlockSpec((ROW_TILE, d), lambda i: (i, 0)),
        out_shape=jax.ShapeDtypeStruct((rows, d), BF16),
        compiler_params=_params("arbitrary"),
        name="rms_cast",
    )(x, g)


def _rms_pair_kernel(a_ref, b_ref, ga_ref, gb_ref, o_ref):
    da = a_ref.shape[1]
    o_ref[:, :da] = _rms(a_ref[...], ga_ref[...]).astype(o_ref.dtype)
    o_ref[:, da:] = _rms(b_ref[...], gb_ref[...]).astype(o_ref.dtype)


def _rms_pair_cast(a, b, ga, gb):
    rows, da = a.shape
    db = b.shape[1]
    return pl.pallas_call(
        _rms_pair_kernel,
        grid=(pl.cdiv(rows, ROW_TILE),),
        in_specs=[pl.BlockSpec((ROW_TILE, da), lambda i: (i, 0)),
                  pl.BlockSpec((ROW_TILE, db), lambda i: (i, 0)),
                  pl.BlockSpec((1, da), lambda i: (0, 0)),
                  pl.BlockSpec((1, db), lambda i: (0, 0))],
        out_specs=pl.BlockSpec((ROW_TILE, da + db), lambda i: (i, 0)),
        out_shape=jax.ShapeDtypeStruct((rows, da + db), BF16),
        compiler_params=_params("arbitrary"),
        name="rms_pair_cast",
    )(a, b, ga, gb)


def _cast_weight_once(w_ref, wb_ref):
    @pl.when(pl.program_id(1) == 0)
    def _():
        wb_ref[...] = w_ref[...].astype(BF16)


def _qkv_kernel(a_ref, w_ref, g_ref, of_ref, ob_ref, wb_ref):
    _cast_weight_once(w_ref, wb_ref)
    z = jnp.dot(a_ref[...], wb_ref[...], preferred_element_type=F32)
    j = pl.program_id(0)

    @pl.when(j < 2)
    def _():
        g = g_ref[...]
        for h in range(z.shape[1] // HEAD_DIM):
            sl = slice(h * HEAD_DIM, (h + 1) * HEAD_DIM)
            y = _rms(z[:, sl], g)
            of_ref[:, h, :] = y
            ob_ref[:, sl] = y.astype(BF16)

    @pl.when(j == 2)
    def _():
        for h in range(z.shape[1] // HEAD_DIM):
            of_ref[:, h, :] = z[:, h * HEAD_DIM:(h + 1) * HEAD_DIM]
        ob_ref[...] = z.astype(BF16)


def _qkv_proj(a, w_in, layer, gains, d_attn):
    rows, d = a.shape
    tm = _mm_row_tile(rows)
    n_heads = d_attn // HEAD_DIM
    out = (jax.ShapeDtypeStruct((3, rows, n_heads, HEAD_DIM), F32), jax.ShapeDtypeStruct((3, rows, d_attn), BF16))
    return pl.pallas_call(
        _qkv_kernel,
        grid=(3, rows // tm),
        in_specs=[pl.BlockSpec((tm, d), lambda j, i: (i, 0)),
                  pl.BlockSpec((None, d, d_attn), lambda j, i: (layer, 0, j)),
                  pl.BlockSpec((None, 1, HEAD_DIM), lambda j, i: (j, 0, 0))],
        out_specs=[pl.BlockSpec((None, tm, n_heads, HEAD_DIM), lambda j, i: (j, i, 0, 0)),
                   pl.BlockSpec((None, tm, d_attn), lambda j, i: (j, i, 0))],
        out_shape=out,
        scratch_shapes=[pltpu.VMEM((d, d_attn), BF16)],
        compiler_params=_params("arbitrary", "arbitrary"),
        name="qkv_proj",
    )(a, w_in, gains)


def _uvs_kernel(a_ref, w_ref, g_ref, o_ref, wb_ref):
    _cast_weight_once(w_ref, wb_ref)
    x = _gelu(jnp.dot(a_ref[...], wb_ref[...], preferred_element_type=F32))
    j = pl.program_id(0)

    @pl.when(j == 0)
    def _():
        o_ref[...] = x

    @pl.when(j == 1)
    def _():
        o_ref[...] = _rms(x, g_ref[...])


def _uvs_proj(a, w_uvs, layer, g_sgu):
    rows, d = a.shape
    d_sgu = w_uvs.shape[2] // 2
    tm = _mm_row_tile(rows)
    return pl.pallas_call(
        _uvs_kernel,
        grid=(2, rows // tm),
        in_specs=[pl.BlockSpec((tm, d), lambda j, i: (i, 0)),
                  pl.BlockSpec((None, d, d_sgu), lambda j, i: (layer, 0, j)),
                  pl.BlockSpec((1, d_sgu), lambda j, i: (0, 0))],
        out_specs=pl.BlockSpec((None, tm, d_sgu), lambda j, i: (j, i, 0)),
        out_shape=jax.ShapeDtypeStruct((2, rows, d_sgu), F32),
        scratch_shapes=[pltpu.VMEM((d, d_sgu), BF16)],
        compiler_params=_params("arbitrary", "arbitrary"),
        name="uvs_proj",
    )(a, w_uvs, g_sgu)


def _mm_res_kernel(a_ref, w_ref, r_ref, o_ref, wb_ref):
    _cast_weight_once(w_ref, wb_ref)
    o_ref[...] = r_ref[...] + jnp.dot(a_ref[...], wb_ref[...], preferred_element_type=F32)


def _mm_residual(a, w, layer, res, name, row_target, col_target):
    rows, k = a.shape
    n = w.shape[2]
    tm = _mm_row_tile(rows, row_target)
    tn = _col_tile(n, col_target)
    return pl.pallas_call(
        _mm_res_kernel,
        grid=(n // tn, rows // tm),
        in_specs=[pl.BlockSpec((tm, k), lambda j, i: (i, 0)),
                  pl.BlockSpec((None, k, tn), lambda j, i: (layer, 0, j)),
                  pl.BlockSpec((tm, tn), lambda j, i: (i, j))],
        out_specs=pl.BlockSpec((tm, tn), lambda j, i: (i, j)),
        out_shape=jax.ShapeDtypeStruct((rows, n), F32),
        scratch_shapes=[pltpu.VMEM((k, tn), BF16)],
        compiler_params=_params("arbitrary", "arbitrary"),
        name=name,
    )(a, w, res)


def _ffn_up_kernel(a_ref, wg_ref, wu_ref, o_ref, wgb_ref, wub_ref):
    _cast_weight_once(wg_ref, wgb_ref)
    _cast_weight_once(wu_ref, wub_ref)
    a = a_ref[...]
    g = jnp.dot(a, wgb_ref[...], preferred_element_type=F32)
    u = jnp.dot(a, wub_ref[...], preferred_element_type=F32)
    o_ref[...] = (g * _sigmoid(g) * u).astype(o_ref.dtype)


def _ffn_up(a, wg, wu, layer):
    rows, k = a.shape
    n = wg.shape[2]
    tm = _mm_row_tile(rows)
    tn = _col_tile(n, 512)
    wspec = pl.BlockSpec((None, k, tn), lambda j, i: (layer, 0, j))
    return pl.pallas_call(
        _ffn_up_kernel,
        grid=(n // tn, rows // tm),
        in_specs=[pl.BlockSpec((tm, k), lambda j, i: (i, 0)), wspec, wspec],
        out_specs=pl.BlockSpec((tm, tn), lambda j, i: (i, j)),
        out_shape=jax.ShapeDtypeStruct((rows, n), BF16),
        scratch_shapes=[pltpu.VMEM((k, tn), BF16), pltpu.VMEM((k, tn), BF16)],
        compiler_params=_params("arbitrary", "arbitrary"),
        name="ffn_up",
    )(a, wg, wu)


def _ple_kernel(a_ref, w_ref, p_ref, wpe_ref, r_ref, o_ref, wb_ref):
    _cast_weight_once(w_ref, wb_ref)
    gate = _sigmoid(jnp.dot(a_ref[...], wb_ref[...], preferred_element_type=F32))
    emb = jnp.dot(p_ref[...].astype(BF16), wpe_ref[...].astype(BF16), preferred_element_type=F32)
    o_ref[...] = r_ref[...] + emb * gate


def _ple(a, w_pg, p, w_pe, layer, res):
    rows, k = a.shape
    n = w_pg.shape[2]
    d_ple = p.shape[1]
    tm = _mm_row_tile(rows)
    tn = _col_tile(n, 1024)
    return pl.pallas_call(
        _ple_kernel,
        grid=(n // tn, rows // tm),
        in_specs=[pl.BlockSpec((tm, k), lambda j, i: (i, 0)),
                  pl.BlockSpec((None, k, tn), lambda j, i: (layer, 0, j)),
                  pl.BlockSpec((tm, d_ple), lambda j, i: (i, 0)),
                  pl.BlockSpec((None, d_ple, tn), lambda j, i: (layer, 0, j)),
                  pl.BlockSpec((tm, tn), lambda j, i: (i, j))],
        out_specs=pl.BlockSpec((tm, tn), lambda j, i: (i, j)),
        out_shape=jax.ShapeDtypeStruct((rows, n), F32),
        scratch_shapes=[pltpu.VMEM((k, tn), BF16)],
        compiler_params=_params("arbitrary", "arbitrary"),
        name="ple",
    )(a, w_pg, p, w_pe, res)


def _forget_kernel(a_ref, wf_ref, wft_ref, bf_ref, bft_ref, lf_ref, cr_ref, carr_ref, *, seq):
    i = pl.program_id(0)
    tm = a_ref.shape[0]
    n_h = lf_ref.shape[1]
    a = a_ref[...]
    lf = _log_sigmoid(jnp.dot(a, wf_ref[...].astype(BF16), preferred_element_type=F32) + bf_ref[...])
    lft = _log_sigmoid(lax.dot_general(wft_ref[...].astype(BF16), a, _NT, preferred_element_type=F32)
                       + bft_ref[...])

    @pl.when((i * tm) % seq == 0)
    def _():
        carr_ref[...] = jnp.zeros_like(carr_ref)

    upto = (lax.broadcasted_iota(I32, (tm, tm), 0) <= lax.broadcasted_iota(I32, (tm, tm), 1)).astype(BF16)
    ct = carr_ref[...]
    for piece in _split3(lft):
        ct = ct + jnp.dot(piece, upto, preferred_element_type=F32)
    carr_ref[...] = ct[:, tm - 1:tm]
    lf_ref[...] = lf[:, :n_h]
    cr_ref[...] = ct


def _forget_gates(a, wf_pad, wf_t, bf_pad, bf_t, seq):
    rows, d = a.shape
    n_h = wf_t.shape[0]
    n_tiles = pl.cdiv(rows, ROW_TILE)
    return pl.pallas_call(
        functools.partial(_forget_kernel, seq=seq),
        grid=(n_tiles,),
        in_specs=[pl.BlockSpec((ROW_TILE, d), lambda i: (i, 0)),
                  pl.BlockSpec((d, V7X_LANES), lambda i: (0, 0)),
                  pl.BlockSpec((n_h, d), lambda i: (0, 0)),
                  pl.BlockSpec((1, V7X_LANES), lambda i: (0, 0)),
                  pl.BlockSpec((n_h, 1), lambda i: (0, 0))],
        out_specs=[pl.BlockSpec((ROW_TILE, n_h), lambda i: (i, 0)),
                   pl.BlockSpec((None, n_h, ROW_TILE), lambda i: (i, 0, 0))],
        out_shape=(jax.ShapeDtypeStruct((rows, n_h), F32),
                   jax.ShapeDtypeStruct((n_tiles, n_h, ROW_TILE), F32)),
        scratch_shapes=[pltpu.VMEM((n_h, 1), F32)],
        compiler_params=_params("arbitrary"),
        name="forget_gates",
    )(a, wf_pad, wf_t, bf_pad, bf_t)


def _sgu_kernel(u_ref, v_ref, w_ref, b_ref, o_ref, *, n_prompt_chunks):
    c = pl.program_id(0)
    n_groups = w_ref.shape[0]
    row = lax.broadcasted_iota(I32, (CHUNK, CHUNK), 0)
    col = lax.broadcasted_iota(I32, (CHUNK, CHUNK), 1)

    @pl.when(c < n_prompt_chunks)
    def _():
        for g in range(n_groups):
            sl = slice(g * GROUP_DIM, (g + 1) * GROUP_DIM)
            wm = jnp.where(col <= row, w_ref[g], 0.0).astype(BF16)
            z = jnp.dot(wm, v_ref[:, sl].astype(BF16), preferred_element_type=F32) + b_ref[:, g:g + 1]
            o_ref[:, sl] = u_ref[:, sl] * z

    @pl.when(c >= n_prompt_chunks)
    def _():
        for g in range(n_groups):
            sl = slice(g * GROUP_DIM, (g + 1) * GROUP_DIM)
            w00 = _bf16_round(w_ref[g][0:1, 0:1])
            z = w00 * _bf16_round(v_ref[:, sl]) + b_ref[0:1, g:g + 1]
            o_ref[:, sl] = u_ref[:, sl] * z


def _sgu(uvs, w_s, b_s_t, layer, n_prompt_rows):
    _, rows, d_sgu = uvs.shape
    n_groups = w_s.shape[1]
    return pl.pallas_call(
        functools.partial(_sgu_kernel, n_prompt_chunks=n_prompt_rows // CHUNK),
        grid=(rows // CHUNK,),
        in_specs=[pl.BlockSpec((None, CHUNK, d_sgu), lambda c: (0, c, 0)),
                  pl.BlockSpec((None, CHUNK, d_sgu), lambda c: (1, c, 0)),
                  pl.BlockSpec((None, n_groups, CHUNK, CHUNK), lambda c: (layer, 0, 0, 0)),
                  pl.BlockSpec((None, CHUNK, n_groups), lambda c: (layer, 0, 0))],
        out_specs=pl.BlockSpec((CHUNK, d_sgu), lambda c: (c, 0)),
        out_shape=jax.ShapeDtypeStruct((rows, d_sgu), F32),
        compiler_params=_params("arbitrary"),
        name="sgu",
    )(uvs, uvs, w_s, b_s_t)


def _attn_kernel(q_ref, k_ref, v_ref, cr_ref, o_ref, m_ref, l_ref):
    qi = pl.program_id(1)
    ts = q_ref.shape[0]
    n_heads = q_ref.shape[1] // HEAD_DIM
    row = lax.broadcasted_iota(I32, (ts, ts), 0)
    col = lax.broadcasted_iota(I32, (ts, ts), 1)
    m_ref[...] = jnp.full_like(m_ref, -jnp.inf)
    l_ref[...] = jnp.zeros_like(l_ref)
    o_ref[...] = jnp.zeros_like(o_ref)

    def key_tile(kt, masked):
        off = pl.multiple_of(kt * ts, ts)
        for h in range(n_heads):
            sl = slice(h * HEAD_DIM, (h + 1) * HEAD_DIM)
            s = lax.dot_general(q_ref[:, sl], k_ref[pl.ds(off, ts), sl], _NT, preferred_element_type=F32)
            s = s - cr_ref[kt, h:h + 1, :]
            if masked:
                s = jnp.where(col <= row, s, -jnp.inf)
            m = m_ref[h]
            m_new = jnp.maximum(m, jnp.max(s, axis=-1, keepdims=True))
            alpha = jnp.exp(m - m_new)
            p = jnp.exp(s - m_new)
            l_ref[h] = alpha * l_ref[h] + jnp.sum(p, axis=-1, keepdims=True)
            o_ref[:, sl] = alpha * o_ref[:, sl] + jnp.dot(p.astype(BF16), v_ref[pl.ds(off, ts), sl],
                                                          preferred_element_type=F32)
            m_ref[h] = m_new

    def body(kt, carry):
        key_tile(kt, False)
        return carry

    lax.fori_loop(0, qi, body, 0)
    key_tile(qi, True)
    for h in range(n_heads):
        sl = slice(h * HEAD_DIM, (h + 1) * HEAD_DIM)
        o_ref[:, sl] = o_ref[:, sl] / l_ref[h]


def _prompt_attention(qkv_b, c_row, batch, seq):
    d_attn = qkv_b.shape[2]
    n_heads = d_attn // HEAD_DIM
    ts = ROW_TILE
    nq = seq // ts
    return pl.pallas_call(
        _attn_kernel,
        grid=(batch, nq),
        in_specs=[pl.BlockSpec((None, ts, d_attn), lambda b, qi: (0, b * nq + qi, 0)),
                  pl.BlockSpec((None, seq, d_attn), lambda b, qi: (1, b, 0)),
                  pl.BlockSpec((None, seq, d_attn), lambda b, qi: (2, b, 0)),
                  pl.BlockSpec((nq, n_heads, ts), lambda b, qi: (b, 0, 0))],
        out_specs=pl.BlockSpec((ts, d_attn), lambda b, qi: (b * nq + qi, 0)),
        out_shape=jax.ShapeDtypeStruct((batch * seq, d_attn), F32),
        scratch_shapes=[pltpu.VMEM((n_heads, ts, 1), F32), pltpu.VMEM((n_heads, ts, 1), F32)],
        compiler_params=_params("arbitrary", "arbitrary"),
        name="prompt_attention",
    )(qkv_b, qkv_b, qkv_b, c_row)


def _page_suffix_kernel(lf_ref, o_ref):
    page = lf_ref.shape[1]
    later = (lax.broadcasted_iota(I32, (page, page), 0) > lax.broadcasted_iota(I32, (page, page), 1)).astype(BF16)
    acc = jnp.zeros(o_ref.shape, F32)
    for piece in _split3(lf_ref[...]):
        acc = acc + jnp.dot(piece, later, preferred_element_type=F32)
    o_ref[...] = acc


def _page_suffix(lf_t):
    depth, rows, page = lf_t.shape
    tile = _row_tile(rows, 1024)
    spec = pl.BlockSpec((None, tile, page), lambda d, i: (d, i, 0))
    return pl.pallas_call(
        _page_suffix_kernel,
        grid=(depth, rows // tile),
        in_specs=[spec],
        out_specs=spec,
        out_shape=jax.ShapeDtypeStruct(lf_t.shape, F32),
        compiler_params=_params("arbitrary", "arbitrary"),
        name="page_suffix",
    )(lf_t)


def _decode_kernel(pt_ref, q_ref, kn_ref, vn_ref, lfn_ref, *rest, n_pp):
    del pt_ref
    k_refs, v_refs = rest[:n_pp], rest[n_pp:2 * n_pp]
    lf_refs, suf_refs = rest[2 * n_pp:3 * n_pp], rest[3 * n_pp:4 * n_pp]
    o_ref, m_ref, l_ref, acc_ref, car_ref = rest[4 * n_pp:]
    g = pl.program_id(1)
    n_h = q_ref.shape[0]
    width = k_refs[0].shape[0]
    own_head = (lax.rem(lax.broadcasted_iota(I32, (n_h, width), 1), n_h)
                == lax.broadcasted_iota(I32, (n_h, width), 0))
    lane_is_head = (lax.broadcasted_iota(I32, (n_h, V7X_LANES), 1)
                    == lax.broadcasted_iota(I32, (n_h, V7X_LANES), 0))
    q = q_ref[...]
    qb = q.astype(BF16)

    @pl.when(g == 0)
    def _():
        m_ref[...] = jnp.full_like(m_ref, -jnp.inf)
        l_ref[...] = jnp.zeros_like(l_ref)
        acc_ref[...] = jnp.zeros_like(acc_ref)
        car_ref[...] = lfn_ref[...]

    car = car_ref[...]
    scores = []
    for r in range(n_pp):
        suf = suf_refs[r][...]
        s = lax.dot_general(qb, k_refs[r][...].astype(BF16), _NT, preferred_element_type=F32)
        scores.append(jnp.where(own_head, s + (suf + car), -jnp.inf))
        page_mass = suf[:, :V7X_LANES] + lf_refs[r][:, :V7X_LANES]
        car = car + jnp.sum(jnp.where(lane_is_head, page_mass, 0.0), axis=-1, keepdims=True)
    car_ref[...] = car
    m = m_ref[...]
    m_new = m
    for s in scores:
        m_new = jnp.maximum(m_new, jnp.max(s, axis=-1, keepdims=True))
    alpha = jnp.exp(m - m_new)
    l = alpha * l_ref[...]
    acc = alpha * acc_ref[...]
    for r in range(n_pp):
        p = jnp.exp(scores[r] - m_new)
        l = l + jnp.sum(p, axis=-1, keepdims=True)
        acc = acc + jnp.dot(p.astype(BF16), v_refs[r][...].astype(BF16), preferred_element_type=F32)
    l_ref[...] = l
    acc_ref[...] = acc
    m_ref[...] = m_new

    @pl.when(g == pl.num_programs(1) - 1)
    def _():
        s_self = jnp.sum(q * _bf16_round(kn_ref[...]), axis=-1, keepdims=True)
        m = m_ref[...]
        m_new = jnp.maximum(m, s_self)
        alpha = jnp.exp(m - m_new)
        p_self = jnp.exp(s_self - m_new)
        l = alpha * l_ref[...] + p_self
        acc = alpha * acc_ref[...] + _bf16_round(p_self) * _bf16_round(vn_ref[...])
        o_ref[...] = acc / l


def _decode_attention(q, k_new, v_new, lf_new, cache_k, cache_v, lf_rows, suf_rows, page_table, layer):
    n_dec, n_h, head_dim = q.shape
    n_pages = page_table.shape[1]
    width = cache_k.shape[2]
    n_pp = PAGES_PER_STEP
    while n_pages % n_pp:
        n_pp //= 2

    def page_map(r):
        return lambda b, g, pt: (layer, pt[b * n_pages + (n_pages - 1 - (g * n_pp + r))], 0, 0)

    vec = pl.BlockSpec((None, n_h, head_dim), lambda b, g, pt: (b, 0, 0))
    in_specs = [vec, vec, vec, pl.BlockSpec((None, n_h, 1), lambda b, g, pt: (b, 0, 0))]
    in_specs += [pl.BlockSpec((None, None, width, head_dim), page_map(r)) for r in range(n_pp)] * 2
    in_specs += [pl.BlockSpec((None, None, 1, width), page_map(r)) for r in range(n_pp)] * 2
    grid_spec = pltpu.PrefetchScalarGridSpec(
        num_scalar_prefetch=1,
        grid=(n_dec, n_pages // n_pp),
        in_specs=in_specs,
        out_specs=vec,
        scratch_shapes=[pltpu.VMEM((n_h, 1), F32), pltpu.VMEM((n_h, 1), F32),
                        pltpu.VMEM((n_h, head_dim), F32), pltpu.VMEM((n_h, 1), F32)],
    )
    return pl.pallas_call(
        functools.partial(_decode_kernel, n_pp=n_pp),
        grid_spec=grid_spec,
        out_shape=jax.ShapeDtypeStruct((n_dec, n_h, head_dim), F32),
        compiler_params=_params("arbitrary", "arbitrary"),
        name="decode_attention",
    )(page_table.reshape(-1), q, k_new, v_new, lf_new,
      *([cache_k] * n_pp), *([cache_v] * n_pp), *([lf_rows] * n_pp), *([suf_rows] * n_pp))


def _route_kernel(h_ref, g_ref, wr_ref, br_ref, c_ref, idx_ref, gate_ref, cnt_ref, *, n_valid_rows):
    i = pl.program_id(0)
    tm = h_ref.shape[0]
    lanes = wr_ref.shape[1]
    c = _rms(h_ref[...], g_ref[...])
    c_ref[...] = c
    logits = jnp.dot(c, wr_ref[...], preferred_element_type=F32, precision=lax.Precision.HIGHEST) + br_ref[...]
    lane = lax.broadcasted_iota(I32, (tm, lanes), 1)
    lg = jnp.where(lane < N_EXPERTS, logits, -jnp.inf)
    m1 = jnp.max(lg, axis=-1, keepdims=True)
    i1 = jnp.min(jnp.where(lg == m1, lane, lanes), axis=-1, keepdims=True)
    lg2 = jnp.where(lane == i1, -jnp.inf, lg)
    m2 = jnp.max(lg2, axis=-1, keepdims=True)
    i2 = jnp.min(jnp.where(lg2 == m2, lane, lanes), axis=-1, keepdims=True)
    e2 = jnp.exp(m2 - m1)
    valid = (i * tm + lax.broadcasted_iota(I32, (tm, 1), 0)) < n_valid_rows
    g1 = jnp.where(valid, 1.0 / (1.0 + e2), 0.0)
    g2 = jnp.where(valid, e2 / (1.0 + e2), 0.0)

    @pl.when(i == 0)
    def _():
        cnt_ref[...] = jnp.zeros_like(cnt_ref)

    sel = jnp.where(((lane == i1) | (lane == i2)) & valid, 1.0, 0.0)
    before = (lax.broadcasted_iota(I32, (tm, tm), 1) < lax.broadcasted_iota(I32, (tm, tm), 0)).astype(BF16)
    cnt = cnt_ref[...]
    rank = jnp.dot(before, sel.astype(BF16), preferred_element_type=F32) + cnt.astype(F32)
    cnt_ref[...] = cnt + jnp.sum(sel, axis=0, keepdims=True).astype(I32)
    r1 = jnp.sum(jnp.where(lane == i1, rank, 0.0), axis=-1, keepdims=True).astype(I32)
    r2 = jnp.sum(jnp.where(lane == i2, rank, 0.0), axis=-1, keepdims=True).astype(I32)
    out_lane = lax.broadcasted_iota(I32, idx_ref.shape, 1)
    idx_ref[...] = jnp.where(out_lane == 0, i1, jnp.where(out_lane == 1, i2, jnp.where(out_lane == 2, r1, r2)))
    gate_ref[...] = jnp.where(out_lane == 0, g1, jnp.where(out_lane == 1, g2, 0.0))


def _route(h, g, wr_pad, br_pad, n_valid_rows):
    rows, d = h.shape
    lanes = wr_pad.shape[1]
    return pl.pallas_call(
        functools.partial(_route_kernel, n_valid_rows=n_valid_rows),
        grid=(pl.cdiv(rows, ROW_TILE),),
        in_specs=[pl.BlockSpec((ROW_TILE, d), lambda i: (i, 0)),
                  pl.BlockSpec((1, d), lambda i: (0, 0)),
                  pl.BlockSpec((d, lanes), lambda i: (0, 0)),
                  pl.BlockSpec((1, lanes), lambda i: (0, 0))],
        out_specs=[pl.BlockSpec((ROW_TILE, d), lambda i: (i, 0)),
                   pl.BlockSpec((ROW_TILE, N_EXPERTS), lambda i: (i, 0)),
                   pl.BlockSpec((ROW_TILE, N_EXPERTS), lambda i: (i, 0)),
                   pl.BlockSpec((1, lanes), lambda i: (0, 0))],
        out_shape=(jax.ShapeDtypeStruct((rows, d), F32),
                   jax.ShapeDtypeStruct((rows, N_EXPERTS), I32),
                   jax.ShapeDtypeStruct((rows, N_EXPERTS), F32),
                   jax.ShapeDtypeStruct((1, lanes), I32)),
        compiler_params=_params("arbitrary"),
        name="route",
    )(h, g, wr_pad, br_pad)


def _row_copy(table_ref, dst_ref, row, r, sem):
    return pltpu.make_async_copy(table_ref.at[pl.ds(row, 1)], dst_ref.at[pl.ds(r, 1)], sem)


def _gather_kernel(idx_ref, n_ref, table_ref, o_ref, buf_ref, sem):
    tg = o_ref.shape[0]
    base = pl.program_id(0) * tg
    wanted = base < n_ref[0]

    @pl.when(wanted)
    def _():
        def start(r, carry):
            _row_copy(table_ref, buf_ref, idx_ref[base + r], r, sem).start()
            return carry

        def wait(r, carry):
            _row_copy(table_ref, buf_ref, 0, r, sem).wait()
            return carry

        lax.fori_loop(0, tg, start, 0, unroll=GATHER_UNROLL)
        lax.fori_loop(0, tg, wait, 0, unroll=GATHER_UNROLL)
        o_ref[...] = buf_ref[...].astype(o_ref.dtype)

    @pl.when(jnp.logical_not(wanted))
    def _():
        o_ref[...] = jnp.zeros_like(o_ref)


def _gather_rows(table, idx, n_wanted, out_dtype):
    n_out = idx.shape[0]
    d = table.shape[1]
    tg = _row_tile(n_out, GATHER_TARGET, BF16_ROWS * GATHER_UNROLL)
    grid_spec = pltpu.PrefetchScalarGridSpec(
        num_scalar_prefetch=2,
        grid=(n_out // tg,),
        in_specs=[pl.BlockSpec(memory_space=pl.ANY)],
        out_specs=pl.BlockSpec((tg, d), lambda i, idx, n: (i, 0)),
        scratch_shapes=[pltpu.VMEM((tg, d), table.dtype), pltpu.SemaphoreType.DMA(())],
    )
    return pl.pallas_call(
        _gather_kernel,
        grid_spec=grid_spec,
        out_shape=jax.ShapeDtypeStruct((n_out, d), out_dtype),
        compiler_params=_params("arbitrary"),
        name="gather_rows",
    )(idx, n_wanted, table)


def _cast_expert_weight(te_ref, w_ref, wb_ref):
    i = pl.program_id(1)

    @pl.when((i == 0) | (te_ref[i] != te_ref[jnp.maximum(i - 1, 0)]))
    def _():
        wb_ref[...] = w_ref[...].astype(BF16)


def _moe_up_kernel(te_ref, nu_ref, x_ref, wg_ref, wu_ref, o_ref, wgb_ref, wub_ref):
    _cast_expert_weight(te_ref, wg_ref, wgb_ref)
    _cast_expert_weight(te_ref, wu_ref, wub_ref)
    used = pl.program_id(1) < nu_ref[0]

    @pl.when(used)
    def _():
        x = x_ref[...]
        g = jnp.dot(x, wgb_ref[...], preferred_element_type=F32)
        u = jnp.dot(x, wub_ref[...], preferred_element_type=F32)
        o_ref[...] = (g * _sigmoid(g) * u).astype(o_ref.dtype)

    @pl.when(jnp.logical_not(used))
    def _():
        o_ref[...] = jnp.zeros_like(o_ref)


def _moe_up(xs, wg, wu, layer, tile_expert, n_used):
    rows, k = xs.shape
    n = wg.shape[3]
    tn = _col_tile(n, 1024)
    wspec = pl.BlockSpec((None, None, k, tn), lambda j, i, te, nu: (layer, te[i], 0, j))
    grid_spec = pltpu.PrefetchScalarGridSpec(
        num_scalar_prefetch=2,
        grid=(n // tn, rows // EXPERT_TILE),
        in_specs=[pl.BlockSpec((EXPERT_TILE, k), lambda j, i, te, nu: (jnp.minimum(i, nu[0] - 1), 0)),
                  wspec, wspec],
        out_specs=pl.BlockSpec((EXPERT_TILE, tn), lambda j, i, te, nu: (i, j)),
        scratch_shapes=[pltpu.VMEM((k, tn), BF16), pltpu.VMEM((k, tn), BF16)],
    )
    return pl.pallas_call(
        _moe_up_kernel,
        grid_spec=grid_spec,
        out_shape=jax.ShapeDtypeStruct((rows, n), BF16),
        compiler_params=_params("arbitrary", "arbitrary"),
        name="moe_up",
    )(tile_expert, n_used, xs, wg, wu)


def _moe_down_kernel(te_ref, nu_ref, x_ref, w_ref, o_ref, wb_ref):
    _cast_expert_weight(te_ref, w_ref, wb_ref)
    used = pl.program_id(1) < nu_ref[0]

    @pl.when(used)
    def _():
        o_ref[...] = jnp.dot(x_ref[...], wb_ref[...], preferred_element_type=F32)

    @pl.when(jnp.logical_not(used))
    def _():
        o_ref[...] = jnp.zeros_like(o_ref)


def _moe_down(hs, wd, layer, tile_expert, n_used):
    rows, k = hs.shape
    n = wd.shape[3]
    tn = _col_tile(n, 512)
    tm = EXPERT_TILE // EXPERT_DOWN_SPLIT
    grid_spec = pltpu.PrefetchScalarGridSpec(
        num_scalar_prefetch=2,
        grid=(n // tn, rows // tm),
        in_specs=[pl.BlockSpec((tm, k), lambda j, i, te, nu: (jnp.minimum(i, nu[0] - 1), 0)),
                  pl.BlockSpec((None, None, k, tn), lambda j, i, te, nu: (layer, te[i], 0, j))],
        out_specs=pl.BlockSpec((tm, tn), lambda j, i, te, nu: (i, j)),
        scratch_shapes=[pltpu.VMEM((k, tn), BF16)],
    )
    return pl.pallas_call(
        _moe_down_kernel,
        grid_spec=grid_spec,
        out_shape=jax.ShapeDtypeStruct((rows, n), F32),
        compiler_params=_params("arbitrary", "arbitrary"),
        name="moe_down",
    )(tile_expert, n_used, hs, wd)


def _combine_kernel(h_ref, y1_ref, y2_ref, gate_ref, g_ref, oh_ref, on_ref):
    gate = gate_ref[...]
    h = h_ref[...] + gate[:, 0:1] * y1_ref[...] + gate[:, 1:2] * y2_ref[...]
    oh_ref[...] = h
    on_ref[...] = _rms(h, g_ref[...]).astype(on_ref.dtype)


def _combine(h, y_pairs, gates, g):
    rows, d = h.shape
    return pl.pallas_call(
        _combine_kernel,
        grid=(pl.cdiv(rows, ROW_TILE),),
        in_specs=[pl.BlockSpec((ROW_TILE, d), lambda i: (i, 0)),
                  pl.BlockSpec((None, ROW_TILE, d), lambda i: (0, i, 0)),
                  pl.BlockSpec((None, ROW_TILE, d), lambda i: (1, i, 0)),
                  pl.BlockSpec((ROW_TILE, N_EXPERTS), lambda i: (i, 0)),
                  pl.BlockSpec((1, d), lambda i: (0, 0))],
        out_specs=[pl.BlockSpec((ROW_TILE, d), lambda i: (i, 0)),
                   pl.BlockSpec((ROW_TILE, d), lambda i: (i, 0))],
        out_shape=(jax.ShapeDtypeStruct((rows, d), F32), jax.ShapeDtypeStruct((rows, d), BF16)),
        compiler_params=_params("arbitrary"),
        name="moe_combine",
    )(h, y_pairs, y_pairs, gates, g)


def _moe_ffn(h, g_ffn, w_router, b_router, wg, wu, wd, layer, n_valid_rows, g_next):
    rows, d = h.shape
    wr_pad = jnp.pad(w_router[layer], ((0, 0), (0, V7X_LANES - N_EXPERTS)))
    br_pad = jnp.pad(b_router[layer], (0, V7X_LANES - N_EXPERTS))[None]
    c, idx, gates, counts = _route(h, g_ffn, wr_pad, br_pad, n_valid_rows)

    n_tiles_max = pl.cdiv(2 * n_valid_rows, EXPERT_TILE) + N_EXPERTS
    n_slots = n_tiles_max * EXPERT_TILE
    counts = counts[0, :N_EXPERTS]
    tiles = (counts + EXPERT_TILE - 1) // EXPERT_TILE
    tile_end = jnp.cumsum(tiles)
    tile_start = tile_end - tiles
    n_used = tile_end[-1:].astype(I32)
    tile_ids = jnp.minimum(jnp.arange(n_tiles_max, dtype=I32), n_used[0] - 1)
    tile_expert = jnp.sum(tile_ids[:, None] >= tile_end[None, :], axis=1).astype(I32)
    valid =(jnp.arange(rows) < n_valid_rows)[:, None]
    slot = tile_start[idx[:, 0:2]] * EXPERT_TILE + idx[:, 2:4]
    token = jnp.broadcast_to(jnp.arange(rows, dtype=I32)[:, None], slot.shape)
    src = (jnp.arange(n_slots, dtype=I32) % rows).at[jnp.where(valid, slot, n_slots).reshape(-1)].set(
        token.reshape(-1), mode="drop")

    xs = _gather_rows(c, src, n_used * EXPERT_TILE, BF16)
    hid = _moe_up(xs, wg, wu, layer, tile_expert, n_used)
    y = _moe_down(hid, wd, layer, jnp.repeat(tile_expert, EXPERT_DOWN_SPLIT),
                  n_used * EXPERT_DOWN_SPLIT)
    back = jnp.where(valid, slot, token % n_slots).astype(I32).T.reshape(-1)
    y_pairs = _gather_rows(y, back, jnp.full((1,), 2 * rows, I32), F32).reshape(2, rows, d)
    return _combine(h, y_pairs, gates, g_next)


def kernel(x_prompt, x_sample, cache_k, cache_v, cache_logf, page_table, p_prompt, p_sample,
           g_mix, w_in, b_f, g_q, g_k, w_s, b_s, g_sgu, g_out_a, g_out_b, w_out, g_ffn,
           w_dense_gate, w_dense_up, w_dense_down, w_router, b_router, w_moe_gate, w_moe_up,
           w_moe_down, g_pe, w_pe, w_pg):
    batch, seq, d_model = x_prompt.shape
    n_dec, dec_seq, _ = x_sample.shape
    assert dec_seq == 1, "the decode attention handles one new token per sequence"
    depth = w_in.shape[0]
    n_pool, page = cache_k.shape[1], cache_k.shape[2]
    n_heads = cache_k.shape[3]
    d_attn = n_heads * HEAD_DIM
    d_sgu = d_model - d_attn
    n_prompt = batch * seq
    n_rows = n_prompt + n_dec
    assert seq % ROW_TILE == 0 and n_prompt % CHUNK == 0
    rows = pl.cdiv(n_rows + CHUNK, ROW_TILE) * ROW_TILE

    h = jnp.concatenate([x_prompt.reshape(n_prompt, d_model), x_sample.reshape(n_dec, d_model),
                         jnp.zeros((rows - n_rows, d_model), F32)])
    p_all = jnp.concatenate([p_prompt.reshape(depth, n_prompt, -1), p_sample.reshape(depth, n_dec, -1),
                             jnp.zeros((depth, rows - n_rows, p_prompt.shape[-1]), F32)], axis=1)

    w_f =w_in[:, :, 3 * d_attn:3 * d_attn + n_heads]
    w_uvs = w_in[:, :, 3 * d_attn + n_heads:]
    wf_pad = jnp.pad(w_f, ((0, 0), (0, 0), (0, V7X_LANES - n_heads)))
    wf_t = jnp.swapaxes(w_f, 1, 2)
    bf_pad = jnp.pad(b_f, ((0, 0), (0, V7X_LANES - n_heads)))
    ones = jnp.ones((depth, HEAD_DIM), F32)
    qkv_gain = jnp.stack([g_q * ATTN_SCALE, g_k, ones], axis=1)[:, :, None, :]
    b_s_t = jnp.swapaxes(b_s, 1, 2)
    cache_k2 = cache_k.reshape(depth, n_pool, page * n_heads, HEAD_DIM)
    cache_v2 = cache_v.reshape(depth, n_pool, page * n_heads, HEAD_DIM)
    lf_rows = cache_logf.reshape(depth, n_pool, 1, page * n_heads)
    suf_t = _page_suffix(jnp.swapaxes(cache_logf, 2, 3).reshape(depth, n_pool * n_heads, page))
    suf_rows = jnp.swapaxes(suf_t.reshape(depth, n_pool, n_heads, page), 2, 3).reshape(depth, n_pool, 1, page * n_heads)

    outs = {name: [] for name in ("kp", "vp", "lp", "ks", "vs", "ls", "us")}
    a = _rms_cast(h, g_mix[0][None])
    for layer in range(depth):
        qkv_f, qkv_b = _qkv_proj(a, w_in, layer, qkv_gain[layer], d_attn)
        logf, c_row = _forget_gates(a, wf_pad[layer], wf_t[layer], bf_pad[layer][None], b_f[layer][:, None], seq)
        uvs = _uvs_proj(a, w_uvs, layer, g_sgu[layer][None])
        sg = _sgu(uvs, w_s, b_s_t, layer, n_prompt)
        att_p = _prompt_attention(qkv_b, c_row, batch, seq)
        dec = slice(n_prompt, n_rows)
        per_head = (n_dec, n_heads, HEAD_DIM)
        att_s = _decode_attention(qkv_b[0, dec].astype(F32).reshape(per_head), qkv_f[1, dec].reshape(per_head),
                                  qkv_f[2, dec].reshape(per_head), logf[dec][:, :, None],
                                  cache_k2, cache_v2, lf_rows, suf_rows, page_table, layer)
        att = jnp.concatenate([att_p, att_s.reshape(n_dec, d_attn), jnp.zeros((rows - n_rows, d_attn), F32)])
        mixed = _rms_pair_cast(att, sg, g_out_a[layer][None], g_out_b[layer][None])
        h = _mm_residual(mixed, w_out, layer, h, "out_proj", MM_ROW_TARGET, 1024)

        outs["kp"].append(qkv_f[1, :n_prompt].reshape(batch, seq, n_heads, HEAD_DIM))
        outs["vp"].append(qkv_f[2, :n_prompt].reshape(batch, seq, n_heads, HEAD_DIM))
        outs["lp"].append(logf[:n_prompt].reshape(batch, seq, n_heads))
        outs["ks"].append(qkv_f[1, dec].reshape(n_dec, 1, n_heads, HEAD_DIM))
        outs["vs"].append(qkv_f[2, dec].reshape(n_dec, 1, n_heads, HEAD_DIM))
        outs["ls"].append(logf[dec].reshape(n_dec, 1, n_heads))
        outs["us"].append(uvs[1, dec].reshape(n_dec, 1, d_sgu))

        if layer % 2 == 0:
            c = _rms_cast(h, g_ffn[layer][None])
            hid = _ffn_up(c, w_dense_gate, w_dense_up, layer // 2)
            h = _mm_residual(hid, w_dense_down, layer // 2, h, "ffn_down", MM_ROW_TARGET // 2, 512)
            a_pe = _rms_cast(h, g_pe[layer][None])
        else:
            h, a_pe = _moe_ffn(h, g_ffn[layer][None], w_router, b_router, w_moe_gate, w_moe_up, w_moe_down,
                               layer // 2, n_rows, g_pe[layer][None])

        h = _ple(a_pe, w_pg, p_all[layer], w_pe, layer, h)
        if layer + 1 < depth:
            a = _rms_cast(h, g_mix[layer + 1][None])

    y_prompt = h[:n_prompt].reshape(batch, seq, d_model)
    y_sample = h[n_prompt:n_rows].reshape(n_dec, 1, d_model)
    return (y_prompt, y_sample, jnp.stack(outs["kp"]), jnp.stack(outs["vp"]), jnp.stack(outs["lp"]),
            jnp.stack(outs["ks"]), jnp.stack(outs["vs"]), jnp.stack(outs["ls"]), jnp.stack(outs["us"]))
```
